```python
import math
import jax
import jax.numpy as jnp
from jax import lax
import numpy as np

D_MODEL = 1024
BATCH = 8
SEQ = 4096
DEPTH = 2
DEC_BATCH = 32
DEC_SEQ = 1
PAST_LEN = 16384
PAGE_SIZE = 128

W_MIX = D_MODEL
HD_A = 64
W_A = D_MODEL // 2
H_A = W_A // HD_A
KVH_A = 2
G_A = H_A // KVH_A
CMP_BLOCK = 32
SEL_BLOCK = 64
N_SEL = 16
WINDOW = 512
NSA_Q_BLOCK = 64
FORCE_BONUS = 1.0e4
W_B = D_MODEL // 4
LRU_BLOCKS = 4
LRU_BW = W_B // LRU_BLOCKS
LRU_CONV = 4
LRU_C = 8.0
W_C = D_MODEL // 4
N_C = 4
DV_C = W_C // N_C
DC = DV_C // 2
ATTN_Q_BLOCK = 128
D_FF = ((8 * D_MODEL // 3 + 127) // 128) * 128
FFN_CONV = 3
ROPE_THETA = 10000.0
EPS = 1e-6
NEG = -1e30

IN_SIZES = (W_A, 3 * 2 * KVH_A * HD_A, 3 * H_A, W_B, W_B, N_C * 2 * DC, N_C * 2 * DC, N_C * DV_C)
IN_SPLITS = tuple(sum(IN_SIZES[:i + 1]) for i in range(len(IN_SIZES) - 1))
N_IN = sum(IN_SIZES)

kernel_name = 'hybrid_nsa_rglru_diffattn_step'


def _rmsnorm(x, g):
    xf = x.astype(jnp.float32)
    y = xf * lax.rsqrt(jnp.mean(xf * xf, axis=-1, keepdims=True) + EPS)
    return (y * g.astype(jnp.float32)).astype(x.dtype)


def _rope(x, pos):
    d = x.shape[-1]
    inv = ROPE_THETA ** (-jnp.arange(0, d, 2, dtype=jnp.float32) / d)
    ang = pos.astype(jnp.float32)[:, None] * inv[None, :]
    shp = (1, pos.shape[0]) + (1,) * (x.ndim - 3) + (d // 2,)
    cos = jnp.cos(ang).reshape(shp)
    sin = jnp.sin(ang).reshape(shp)
    xf = x.astype(jnp.float32)
    x1, x2 = xf[..., :d // 2], xf[..., d // 2:]
    return jnp.concatenate([x1 * cos - x2 * sin, x2 * cos + x1 * sin], axis=-1).astype(x.dtype)


def _masked_softmax(s, mask):
    s = jnp.where(mask, s.astype(jnp.float32), NEG)
    p = jax.nn.softmax(s, axis=-1)
    return jnp.where(mask, p, 0.0)


def _causal_conv(x, buf, w, b):
    K = w.shape[0]
    L = x.shape[1]
    xp = jnp.concatenate([buf.astype(x.dtype), x], axis=1)
    y = b
    for k in range(K):
        y = y + xp[:, k:k + L] * w[k]
    return y, xp[:, L:]


def _compress(k, pe, w1, w2):
    B, T = k.shape[:2]
    kb = k.reshape(B, T // CMP_BLOCK, CMP_BLOCK, KVH_A, HD_A) + pe[None, None, :, None, :]
    hid = jax.nn.silu(jnp.einsum('bcpgd,pde->bcge', kb, w1))
    return jnp.einsum('bcge,ef->bcgf', hid, w2)


def _nsa_cmp_slc(q, qpos, ck, cv, ksb, vsb):
    B, Lq = q.shape[:2]
    n_c = ck.shape[1]
    n_s = ksb.shape[2]
    scale = HD_A ** -0.5
    qg = q.reshape(B, Lq, KVH_A, G_A, HD_A)
    s = jnp.einsum('bqgrd,bcgd->bqgrc', qg, ck) * scale
    c_end = (jnp.arange(n_c) + 1) * CMP_BLOCK - 1
    m_c = (c_end[None, :] <= qpos[:, None])[None, :, None, None, :]
    p_c = _masked_softmax(s, m_c)
    o_c = jnp.einsum('bqgrc,bcgd->bqgrd', p_c.astype(cv.dtype), cv)
    imp = p_c.sum(axis=3).reshape(B, Lq, KVH_A, n_s, SEL_BLOCK // CMP_BLOCK).sum(axis=-1)
    blk = jnp.arange(n_s)[None, :]
    cur = (qpos // SEL_BLOCK)[:, None]
    valid = blk * SEL_BLOCK <= qpos[:, None]
    forced = (blk == 0) | (blk == cur) | (blk == cur - 1)
    score = jnp.where(valid[None, :, None, :],
                      imp + jnp.where(forced, FORCE_BONUS, 0.0)[None, :, None, :],
                      -jnp.inf)
    n_top = min(N_SEL, n_s)
    _, idx = lax.top_k(score, n_top)
    b_ix = jnp.arange(B)[:, None, None, None]
    g_ix = jnp.arange(KVH_A)[None, None, :, None]
    kg = ksb[b_ix, g_ix, idx].reshape(B, Lq, KVH_A, n_top * SEL_BLOCK, HD_A)
    vg = vsb[b_ix, g_ix, idx].reshape(B, Lq, KVH_A, n_top * SEL_BLOCK, HD_A)
    kpos = (idx[..., None] * SEL_BLOCK + jnp.arange(SEL_BLOCK)).reshape(B, Lq, KVH_A, n_top * SEL_BLOCK)
    m_s = (kpos <= qpos[None, :, None, None])[:, :, :, None, :]
    s = jnp.einsum('bqgrd,bqgkd->bqgrk', qg, kg) * scale
    p_s = _masked_softmax(s, m_s)
    o_s = jnp.einsum('bqgrk,bqgkd->bqgrd', p_s.astype(vg.dtype), vg)
    return o_c.reshape(B, Lq, H_A, HD_A), o_s.reshape(B, Lq, H_A, HD_A)


def _window_attn(q, qpos, k, v, kpos):
    B, Lq = q.shape[:2]
    qg = q.reshape(B, Lq, KVH_A, G_A, HD_A)
    s = jnp.einsum('bqgrd,bkgd->bqgrk', qg, k) * HD_A ** -0.5
    rel = qpos[:, None] - kpos[None, :]
    m = ((rel >= 0) & (rel < WINDOW) & (kpos[None, :] >= 0))[None, :, None, None, :]
    p = _masked_softmax(s, m)
    return jnp.einsum('bqgrk,bkgd->bqgrd', p.astype(v.dtype), v).reshape(B, Lq, H_A, HD_A)


def _diff_attn(q, qpos, k, v, kpos, lam):
    s = jnp.einsum('bqhcd,bkhcd->bhcqk', q, k) * DC ** -0.5
    m = (kpos[None, :] <= qpos[:, None])[None, None, None]
    p = _masked_softmax(s, m)
    a = p[:, :, 0] - lam * p[:, :, 1]
    return jnp.einsum('bhqk,bkhd->bqhd', a.astype(v.dtype), v)


def _lru_scan(a, u, h0):
    u = u.at[:, 0].add(a[:, 0] * h0)

    def combine(left, right):
        a1, b1 = left
        a2, b2 = right
        return a1 * a2, a2 * b1 + b2

    _, h = lax.associative_scan(combine, (a, u), axis=1)
    return h


def _gather_pages(pool, l, page_table):
    g = pool[l, page_table]
    return g.reshape((g.shape[0], g.shape[1] * g.shape[2]) + g.shape[3:])


def _layer(x, start, p, past):
    (norm_mix, w_in, cmp_pe, cmp_w1, cmp_w2, gn_nsa,
     lru_conv_w, lru_conv_b, lru_gate_a_w, lru_gate_a_b,
     lru_gate_x_w, lru_gate_x_b, lru_lambda, gn_lru,
     diff_lambda, diff_subln, w_out, norm_ffn, w_up,
     ffn_conv_w, ffn_conv_b, w_down, lam_init) = p
    B, L, _ = x.shape
    pos = start + jnp.arange(L)
    h = _rmsnorm(x, norm_mix)
    z = h @ w_in
    q_a, kv_a, g_a, x_b, y_b, q_c, k_c, v_c = jnp.split(z, IN_SPLITS, axis=-1)

    q_a = _rope(q_a.reshape(B, L, H_A, HD_A), pos)
    kv_a = kv_a.reshape(B, L, 3, 2, KVH_A, HD_A)
    kv_a = jnp.stack([_rope(kv_a[:, :, :, 0], pos), kv_a[:, :, :, 1]], axis=3)
    new_nsa = kv_a[:, :, :2].reshape(B, L, 4, KVH_A, HD_A)
    new_win = kv_a[:, :, 2]
    L_pad = -(-L // SEL_BLOCK) * SEL_BLOCK
    tail = jnp.pad(new_nsa, ((0, 0), (0, L_pad - L), (0, 0), (0, 0), (0, 0)))
    full = tail if past is None else jnp.concatenate([past[0], tail], axis=1)
    n_s = full.shape[1] // SEL_BLOCK
    ck = _compress(full[:, :, 0], cmp_pe[0], cmp_w1[0], cmp_w2[0])
    cv = _compress(full[:, :, 1], cmp_pe[1], cmp_w1[1], cmp_w2[1])
    ksb = full[:, :, 2].reshape(B, n_s, SEL_BLOCK, KVH_A, HD_A).transpose(0, 3, 1, 2, 4)
    vsb = full[:, :, 3].reshape(B, n_s, SEL_BLOCK, KVH_A, HD_A).transpose(0, 3, 1, 2, 4)
    if past is None:
        wpad = jnp.pad(new_win, ((0, 0), (WINDOW, 0), (0, 0), (0, 0), (0, 0)))

        def nsa_block(i):
            s0 = i * NSA_Q_BLOCK
            qb = lax.dynamic_slice_in_dim(q_a, s0, NSA_Q_BLOCK, axis=1)
            qp = s0 + jnp.arange(NSA_Q_BLOCK)
            oc, osl = _nsa_cmp_slc(qb, qp, ck, cv, ksb, vsb)
            wb = lax.dynamic_slice_in_dim(wpad, s0, WINDOW + NSA_Q_BLOCK, axis=1)
            kp = s0 - WINDOW + jnp.arange(WINDOW + NSA_Q_BLOCK)
            ow = _window_attn(qb, qp, wb[:, :, 0], wb[:, :, 1], kp)
            return jnp.stack([oc, osl, ow], axis=3)

        o3 = lax.map(nsa_block, jnp.arange(L // NSA_Q_BLOCK))
        o3 = jnp.moveaxis(o3, 0, 1).reshape(B, L, H_A, 3, HD_A)
        win_state = new_win[:, L - min(WINDOW, L):]
    else:
        oc, osl = _nsa_cmp_slc(q_a, pos, ck, cv, ksb, vsb)
        n_buf = past[3].shape[1]
        wkv = jnp.concatenate([past[3].astype(new_win.dtype), new_win], axis=1)
        kp = start - n_buf + jnp.arange(n_buf + L)
        ow = _window_attn(q_a, pos, wkv[:, :, 0], wkv[:, :, 1], kp)
        o3 = jnp.stack([oc, osl, ow], axis=3)
        win_state = new_win
    gate = jax.nn.sigmoid(g_a.astype(jnp.float32)).reshape(B, L, H_A, 3, 1)
    o_a = _rmsnorm((o3 * gate).sum(axis=3).reshape(B, L, W_A), gn_nsa)

    buf_b = jnp.zeros((B, LRU_CONV - 1, W_B), x.dtype) if past is None else past[5]
    xc, lru_conv_state = _causal_conv(x_b, buf_b, lru_conv_w, lru_conv_b)
    xc = xc.astype(jnp.float32)
    xg = xc.reshape(B, L, LRU_BLOCKS, LRU_BW)
    r = jax.nn.sigmoid(jnp.einsum('blnc,ncd->blnd', xg, lru_gate_a_w).reshape(B, L, W_B) + lru_gate_a_b)
    ig = jax.nn.sigmoid(jnp.einsum('blnc,ncd->blnd', xg, lru_gate_x_w).reshape(B, L, W_B) + lru_gate_x_b)
    log_a = -LRU_C * r * jax.nn.softplus(-lru_lambda.astype(jnp.float32))
    a = jnp.exp(log_a)
    u = jnp.sqrt(-jnp.expm1(2.0 * log_a)) * (ig * xc)
    h0 = jnp.zeros((B, W_B), jnp.float32) if past is None else past[4].astype(jnp.float32)
    hseq = _lru_scan(a, u, h0)
    o_b = _rmsnorm(jax.nn.gelu(y_b.astype(jnp.float32)) * hseq, gn_lru)
    lru_h_state = hseq[:, -1].astype(x.dtype)

    q_c = _rope(q_c.reshape(B, L, N_C, 2, DC), pos)
    k_c = _rope(k_c.reshape(B, L, N_C, 2, DC), pos)
    v_c = v_c.reshape(B, L, N_C, DV_C)
    dl = diff_lambda.astype(jnp.float32)
    lam = jnp.exp(jnp.sum(dl[0] * dl[1])) - jnp.exp(jnp.sum(dl[2] * dl[3])) + lam_init
    if past is None:
        def diff_block(i):
            s0 = i * ATTN_Q_BLOCK
            qb = lax.dynamic_slice_in_dim(q_c, s0, ATTN_Q_BLOCK, axis=1)
            return _diff_attn(qb, s0 + jnp.arange(ATTN_Q_BLOCK), k_c, v_c, pos, lam)

        od = lax.map(diff_block, jnp.arange(L // ATTN_Q_BLOCK))
        od = jnp.moveaxis(od, 0, 1).reshape(B, L, N_C, DV_C)
    else:
        kf = jnp.concatenate([past[1].astype(k_c.dtype), k_c], axis=1)
        vf = jnp.concatenate([past[2].astype(v_c.dtype), v_c], axis=1)
        od = _diff_attn(q_c, pos, kf, vf, jnp.arange(start + L), lam)
    o_c = (_rmsnorm(od, diff_subln) * (1.0 - lam_init)).reshape(B, L, W_C)

    mix = jnp.concatenate([o_a.astype(x.dtype), o_b.astype(x.dtype), o_c.astype(x.dtype)], axis=-1)
    x = x + mix @ w_out

    h2 = _rmsnorm(x, norm_ffn)
    g_f, v_f = jnp.split(h2 @ w_up, [D_FF], axis=-1)
    buf_f = jnp.zeros((B, FFN_CONV - 1, D_FF), x.dtype) if past is None else past[6]
    g_f, ffn_state = _causal_conv(g_f, buf_f, ffn_conv_w, ffn_conv_b)
    x = x + (jax.nn.gelu(g_f) * v_f) @ w_down
    return x, (new_nsa, win_state, k_c, v_c, lru_h_state, lru_conv_state, ffn_state)


def _stack(states, j):
    return jnp.stack([s[j] for s in states], axis=0)


def setup_inputs(seed: int = 0) -> dict:
    key = jax.random.key(seed)
    ks = jax.random.split(key, 40)
    f32 = jnp.float32
    n_pages = PAST_LEN // PAGE_SIZE
    n_used = DEC_BATCH * n_pages
    n_pool = (n_used * 5) // 4
    win_buf = min(WINDOW, PAST_LEN)

    def nrm(k, shape, scale=1.0):
        return scale * jax.random.normal(k, shape, f32)

    def gain(k, shape):
        return 1.0 + 0.05 * jax.random.normal(k, shape, f32)

    u = jax.random.uniform(ks[22], (DEPTH, W_B), f32, 0.9, 0.999)
    a0 = u ** (1.0 / LRU_C)
    lru_lambda = jnp.log(a0) - jnp.log1p(-a0)
    page_table = jax.random.permutation(ks[9], n_pool)[:n_used].reshape(DEC_BATCH, n_pages).astype(jnp.int32)
    return {
        'x_prompt': nrm(ks[0], (BATCH, SEQ, D_MODEL)),
        'x_sample': nrm(ks[1], (DEC_BATCH, DEC_SEQ, D_MODEL)),
        'cache_nsa_kv': nrm(ks[2], (DEPTH, n_pool, PAGE_SIZE, 4, KVH_A, HD_A)),
        'cache_diff_k': nrm(ks[3], (DEPTH, n_pool, PAGE_SIZE, N_C, 2, DC)),
        'cache_diff_v': nrm(ks[4], (DEPTH, n_pool, PAGE_SIZE, N_C, DV_C)),
        'cache_win_kv': nrm(ks[5], (DEPTH, DEC_BATCH, win_buf, 2, KVH_A, HD_A)),
        'state_lru_h': nrm(ks[6], (DEPTH, DEC_BATCH, W_B), 0.5),
        'state_lru_conv': nrm(ks[7], (DEPTH, DEC_BATCH, LRU_CONV - 1, W_B)),
        'state_ffn_conv': nrm(ks[8], (DEPTH, DEC_BATCH, FFN_CONV - 1, D_FF)),
        'page_table': page_table,
        'norm_mix': gain(ks[10], (DEPTH, D_MODEL)),
        'w_in': nrm(ks[11], (DEPTH, D_MODEL, N_IN), D_MODEL ** -0.5),
        'cmp_pe': nrm(ks[12], (DEPTH, 2, CMP_BLOCK, HD_A), 0.1),
        'cmp_w1': nrm(ks[13], (DEPTH, 2, CMP_BLOCK, HD_A, HD_A), (CMP_BLOCK * HD_A) ** -0.5),
        'cmp_w2': nrm(ks[14], (DEPTH, 2, HD_A, HD_A), HD_A ** -0.5),
        'gn_nsa': gain(ks[15], (DEPTH, W_A)),
        'lru_conv_w': nrm(ks[16], (DEPTH, LRU_CONV, W_B), LRU_CONV ** -0.5),
        'lru_conv_b': nrm(ks[17], (DEPTH, W_B), 0.01),
        'lru_gate_a_w': nrm(ks[18], (DEPTH, LRU_BLOCKS, LRU_BW, LRU_BW), LRU_BW ** -0.5),
        'lru_gate_a_b': nrm(ks[19], (DEPTH, W_B), 0.01),
        'lru_gate_x_w': nrm(ks[20], (DEPTH, LRU_BLOCKS, LRU_BW, LRU_BW), LRU_BW ** -0.5),
        'lru_gate_x_b': nrm(ks[21], (DEPTH, W_B), 0.01),
        'lru_lambda': lru_lambda,
        'gn_lru': gain(ks[23], (DEPTH, W_B)),
        'diff_lambda': nrm(ks[24], (DEPTH, 4, DC), 0.1),
        'diff_subln': gain(ks[25], (DEPTH, DV_C)),
        'w_out': nrm(ks[26], (DEPTH, W_MIX, D_MODEL), W_MIX ** -0.5),
        'norm_ffn': gain(ks[27], (DEPTH, D_MODEL)),
        'w_up': nrm(ks[28], (DEPTH, D_MODEL, 2 * D_FF), D_MODEL ** -0.5),
        'ffn_conv_w': nrm(ks[29], (DEPTH, FFN_CONV, D_FF), FFN_CONV ** -0.5),
        'ffn_conv_b': nrm(ks[30], (DEPTH, D_FF), 0.01),
        'w_down': nrm(ks[31], (DEPTH, D_FF, D_MODEL), D_FF ** -0.5),
        'norm_final': gain(ks[32], (D_MODEL,)),
    }


def reference(x_prompt, x_sample, cache_nsa_kv, cache_diff_k, cache_diff_v, cache_win_kv,
              state_lru_h, state_lru_conv, state_ffn_conv, page_table,
              norm_mix, w_in, cmp_pe, cmp_w1, cmp_w2, gn_nsa,
              lru_conv_w, lru_conv_b, lru_gate_a_w, lru_gate_a_b, lru_gate_x_w, lru_gate_x_b,
              lru_lambda, gn_lru, diff_lambda, diff_subln, w_out, norm_ffn, w_up,
              ffn_conv_w, ffn_conv_b, w_down, norm_final):
    xp = x_prompt
    xs = x_sample
    st_p = []
    st_s = []
    for l in range(DEPTH):
        lam_init = 0.8 - 0.6 * math.exp(-0.3 * l)
        p = (norm_mix[l], w_in[l], cmp_pe[l], cmp_w1[l], cmp_w2[l], gn_nsa[l],
             lru_conv_w[l], lru_conv_b[l], lru_gate_a_w[l], lru_gate_a_b[l],
             lru_gate_x_w[l], lru_gate_x_b[l], lru_lambda[l], gn_lru[l],
             diff_lambda[l], diff_subln[l], w_out[l], norm_ffn[l], w_up[l],
             ffn_conv_w[l], ffn_conv_b[l], w_down[l], lam_init)
        xp, sp = _layer(xp, 0, p, None)
        past = (_gather_pages(cache_nsa_kv, l, page_table),
                _gather_pages(cache_diff_k, l, page_table),
                _gather_pages(cache_diff_v, l, page_table),
                cache_win_kv[l], state_lru_h[l], state_lru_conv[l], state_ffn_conv[l])
        xs, ss = _layer(xs, PAST_LEN, p, past)
        st_p.append(sp)
        st_s.append(ss)
    y_prompt = _rmsnorm(xp, norm_final)
    y_sample = _rmsnorm(xs, norm_final)
    p_nsa_kv = _stack(st_p, 0)
    p_win_kv = _stack(st_p, 1)
    p_diff_k = _stack(st_p, 2)
    p_diff_v = _stack(st_p, 3)
    p_lru_h = _stack(st_p, 4)
    p_lru_conv = _stack(st_p, 5)
    p_ffn_conv = _stack(st_p, 6)
    s_nsa_kv = _stack(st_s, 0)
    s_win_kv = _stack(st_s, 1)
    s_diff_k = _stack(st_s, 2)
    s_diff_v = _stack(st_s, 3)
    s_lru_h = _stack(st_s, 4)
    s_lru_conv = _stack(st_s, 5)
    s_ffn_conv = _stack(st_s, 6)
    return (y_prompt, y_sample,
            p_nsa_kv, p_win_kv, p_diff_k, p_diff_v, p_lru_h, p_lru_conv, p_ffn_conv,
            s_nsa_kv, s_win_kv, s_diff_k, s_diff_v, s_lru_h, s_lru_conv, s_ffn_conv)
```

```python
import functools
import math

import jax
import jax.numpy as jnp
from jax import lax
from jax.experimental import pallas as pl
from jax.experimental.pallas import tpu as pltpu

F32 = jnp.float32
BF16 = jnp.bfloat16

D_MODEL = 1024
HD_A = 64
W_A = D_MODEL // 2
H_A = W_A // HD_A
KVH_A = 2
G_A = H_A // KVH_A
CMP_BLOCK = 32
SEL_BLOCK = 64
N_SEL = 16
WINDOW = 512
FORCE_BONUS = 1.0e4
W_B = D_MODEL // 4
LRU_BLOCKS = 4
LRU_CONV = 4
LRU_C = 8.0
W_C = D_MODEL // 4
N_C = 4
DV_C = W_C // N_C
DC = DV_C // 2
D_FF = ((8 * D_MODEL // 3 + 127) // 128) * 128
FFN_CONV = 3
ROPE_THETA = 10000.0
EPS = 1e-6
NEG = -1e30

LANES = 128
SUBLANES = 8
N_GATE = 3 * H_A
GATE_PAD = LANES
N_Z = W_A + 3 * 2 * KVH_A * HD_A + 5 * W_B + GATE_PAD
VMEM_LIMIT = 56 * 1024 * 1024


def _cparams(n_axes):
    return pltpu.CompilerParams(dimension_semantics=("arbitrary",) * n_axes,
                                vmem_limit_bytes=VMEM_LIMIT)


def _rms(x, g):
    return x * lax.rsqrt(jnp.mean(x * x, axis=-1, keepdims=True) + EPS) * g


def _dot(a, b):
    return jnp.dot(a.astype(BF16), b.astype(BF16), preferred_element_type=F32)


def _dot_nt(a, b):
    return lax.dot_general(a.astype(BF16), b.astype(BF16), (((1,), (1,)), ((), ())),
                           preferred_element_type=F32)


def _iota(shape, dim):
    return lax.broadcasted_iota(jnp.int32, shape, dim)


def _gelu(x):
    return jax.nn.gelu(x)


def _rope_tables(pos, d):
    inv = ROPE_THETA ** (-jnp.arange(0, d, 2, dtype=F32) / d)
    ang = pos.astype(F32)[:, None] * inv[None, :]
    cos, sin = jnp.cos(ang), jnp.sin(ang)
    zero = jnp.zeros_like(sin)
    rep = LANES // d
    cos_t = jnp.tile(jnp.concatenate([cos, cos], axis=1), (1, rep))
    sin_lo = jnp.tile(jnp.concatenate([-sin, zero], axis=1), (1, rep))
    sin_hi = jnp.tile(jnp.concatenate([zero, sin], axis=1), (1, rep))
    return cos_t, sin_lo, sin_hi


def _inproj_kernel(x_ref, g_ref, w_ref, c64, sl64, sh64, c32, sl32, sh32,
                   qa_ref, nsa_ref, win_ref, xb_ref, yb_ref, qc_ref, kc_ref, vc_ref, ga_ref):
    h = _rms(x_ref[...], g_ref[...]).astype(BF16)

    def seg(a, n):
        return jnp.dot(h, w_ref[:, a:a + n], preferred_element_type=F32)

    def rope(z, half, tabs):
        cos, s_lo, s_hi = tabs[0][...], tabs[1][...], tabs[2][...]
        return z * cos + pltpu.roll(z, LANES - half, 1) * s_lo + pltpu.roll(z, half, 1) * s_hi

    t64 = (c64, sl64, sh64)
    t32 = (c32, sl32, sh32)
    col = 0
    plan = ((qa_ref, (32, 32, 32, 32)), (nsa_ref, (32, None, 32, None)), (win_ref, (32, None)),
            (xb_ref, (None, None)), (yb_ref, (None, None)), (qc_ref, (16, 16)), (kc_ref, (16, 16)),
            (vc_ref, (None, None)), (ga_ref, (None,)))
    for out_ref, chunks in plan:
        z = seg(col, LANES * len(chunks))
        for c, half in enumerate(chunks):
            zc = z[:, c * LANES:(c + 1) * LANES]
            if half is not None:
                zc = rope(zc, half, t64 if half == 32 else t32)
            out_ref[:, c * LANES:(c + 1) * LANES] = zc
        col += LANES * len(chunks)


def _inproj(x2d, gain, w_r, tabs64, tabs32, seq_len, tm):
    n = x2d.shape[0]
    lt = tabs64[0].shape[0]
    if lt == 1:
        tab_spec = pl.BlockSpec((1, LANES), lambda i: (0, 0))
    else:
        per_seq = seq_len // tm
        tab_spec = pl.BlockSpec((tm, LANES), lambda i: (i % per_seq, 0))
    widths = (W_A, 4 * KVH_A * HD_A, 2 * KVH_A * HD_A, W_B, W_B, W_C, W_C, W_C, GATE_PAD)
    out_shape = tuple(jax.ShapeDtypeStruct((n, w), F32) for w in widths)
    out_specs = tuple(pl.BlockSpec((tm, w), lambda i: (i, 0)) for w in widths)
    return pl.pallas_call(
        _inproj_kernel,
        grid=(n // tm,),
        in_specs=[pl.BlockSpec((tm, D_MODEL), lambda i: (i, 0)),
                  pl.BlockSpec((1, D_MODEL), lambda i: (0, 0)),
                  pl.BlockSpec((D_MODEL, N_Z), lambda i: (0, 0))] + [tab_spec] * 6,
        out_specs=out_specs,
        out_shape=out_shape,
        compiler_params=_cparams(1),
        name="inproj",
    )(x2d, gain, w_r, *tabs64, *tabs32)


def _compress_rows(tok_refs, n_blocks, pe_ref, w1_ref, w2_ref):
    outs = []
    for kind, tok_ref in enumerate(tok_refs):
        xs = [(tok_ref[pl.ds(p, n_blocks, stride=CMP_BLOCK), :] + pe_ref[kind, p:p + 1, :]).astype(BF16)
              for p in range(CMP_BLOCK)]
        acc = jnp.dot(jnp.concatenate(xs, axis=1), w1_ref[kind], preferred_element_type=F32)
        hid = acc * jax.nn.sigmoid(acc)
        outs.append(jnp.dot(hid.astype(BF16), w2_ref[kind], preferred_element_type=F32))
    return outs


def _compress_kernel(k_ref, v_ref, pe_ref, w1_ref, w2_ref, out_ref):
    kvw = KVH_A * HD_A
    ck, cv = _compress_rows((k_ref, v_ref), out_ref.shape[0], pe_ref, w1_ref, w2_ref)
    out_ref[:, 0:kvw] = ck
    out_ref[:, kvw:2 * kvw] = cv


def _compress_weight_specs(const):
    kvw = KVH_A * HD_A
    return [pl.BlockSpec((2, CMP_BLOCK, kvw), const(3)),
            pl.BlockSpec((2, CMP_BLOCK * kvw, kvw), const(3)),
            pl.BlockSpec((2, kvw, kvw), const(3))]


def _compress(tok2d, pe2, w1bd, w2bd, t_tile):
    n_tok = tok2d.shape[0]
    kvw = KVH_A * HD_A
    return pl.pallas_call(
        _compress_kernel,
        grid=(n_tok // t_tile,),
        in_specs=[pl.BlockSpec((t_tile, kvw), lambda i: (i, 0)),
                  pl.BlockSpec((t_tile, kvw), lambda i: (i, 1))]
                 + _compress_weight_specs(lambda nd: (lambda i: (0,) * nd)),
        out_specs=pl.BlockSpec((t_tile // CMP_BLOCK, 2 * kvw), lambda i: (i, 0)),
        out_shape=jax.ShapeDtypeStruct((n_tok // CMP_BLOCK, 2 * kvw), F32),
        compiler_params=_cparams(1),
        name="compress",
    )(tok2d, tok2d, pe2, w1bd, w2bd)


def _compress_paged_kernel(layer, n_pages, chunk, pt_ref, cache_ref, pe_ref, w1_ref, w2_ref,
                           out_ref, buf, sem):
    b = pl.program_id(0)
    n_chunks = n_pages // chunk
    page_rows = buf.shape[1] // chunk
    kvw = KVH_A * HD_A
    blocks_per_chunk = chunk * page_rows // CMP_BLOCK

    def page_copy(page, slot, j, kind):
        return pltpu.make_async_copy(cache_ref.at[layer, page, :, pl.ds(kind * kvw, kvw)],
                                     buf.at[2 * slot + kind, pl.ds(j * page_rows, page_rows), :],
                                     sem.at[slot])

    def start(c, slot):
        for j in range(chunk):
            page = pt_ref[b * n_pages + c * chunk + j]
            for kind in range(2):
                page_copy(page, slot, j, kind).start()

    def wait(slot):
        for j in range(chunk):
            for kind in range(2):
                page_copy(0, slot, j, kind).wait()

    start(0, 0)
    for c in range(n_chunks):
        slot = c % 2
        if c + 1 < n_chunks:
            start(c + 1, 1 - slot)
        wait(slot)
        ck, cv = _compress_rows((buf.at[2 * slot], buf.at[2 * slot + 1]), blocks_per_chunk,
                                pe_ref, w1_ref, w2_ref)
        rows = slice(c * blocks_per_chunk, (c + 1) * blocks_per_chunk)
        out_ref[0, rows, 0:kvw] = ck
        out_ref[0, rows, kvw:2 * kvw] = cv


def _compress_paged(cache4, layer, page_table_flat, n_seq, n_pages, pe2, w1bd, w2bd, chunk):
    page_rows = cache4.shape[2]
    kvw = KVH_A * HD_A
    n_blocks = n_pages * page_rows // CMP_BLOCK
    grid_spec = pltpu.PrefetchScalarGridSpec(
        num_scalar_prefetch=1,
        grid=(n_seq,),
        in_specs=[pl.BlockSpec(memory_space=pl.ANY)]
                 + _compress_weight_specs(lambda nd: (lambda b, pt: (0,) * nd)),
        out_specs=pl.BlockSpec((1, n_blocks, 2 * kvw), lambda b, pt: (b, 0, 0)),
        scratch_shapes=[pltpu.VMEM((4, chunk * page_rows, kvw), F32),
                        pltpu.SemaphoreType.DMA((2,))],
    )
    return pl.pallas_call(
        functools.partial(_compress_paged_kernel, layer, n_pages, chunk),
        grid_spec=grid_spec,
        out_shape=jax.ShapeDtypeStruct((n_seq, n_blocks, 2 * kvw), F32),
        compiler_params=_cparams(1),
        name="compress_paged",
    )(page_table_flat, cache4, pe2, w1bd, w2bd)


def _cmp_branch(qg, ck_e, ck_o, cv_e, cv_o, m_e, m_o, n_rep):
    tq, n_s = m_e.shape
    single = tq == 1
    split = (lambda x: x) if single else (lambda x: x.reshape(n_rep, tq, n_s))
    merge = (lambda x: x) if single else (lambda x: x.reshape(n_rep * tq, n_s))
    if not single:
        m_e, m_o = m_e[None], m_o[None]
    s_e = jnp.where(m_e, split(_dot_nt(qg, ck_e)), NEG)
    s_o = jnp.where(m_o, split(_dot_nt(qg, ck_o)), NEG)
    mx = jnp.maximum(jnp.max(s_e, axis=-1, keepdims=True), jnp.max(s_o, axis=-1, keepdims=True))
    e_e = jnp.exp(s_e - mx)
    e_o = jnp.exp(s_o - mx)
    den = jnp.sum(e_e, axis=-1, keepdims=True) + jnp.sum(e_o, axis=-1, keepdims=True)
    p_e = jnp.where(m_e, e_e / den, 0.0)
    p_o = jnp.where(m_o, e_o / den, 0.0)
    o_c = _dot(merge(p_e), cv_e) + _dot(merge(p_o), cv_o)
    imp = jnp.sum(p_e, axis=0, keepdims=single) + jnp.sum(p_o, axis=0, keepdims=single)
    return o_c, imp


def _block_scores(imp, qpos, blk):
    cur = qpos // SEL_BLOCK
    valid = blk * SEL_BLOCK <= qpos
    forced = (blk == 0) | (blk == cur) | (blk == cur - 1)
    return jnp.where(valid, imp + jnp.where(forced, FORCE_BONUS, 0.0), -jnp.inf)


def _gate_mix_norm(o_c, o_s, o_w, gate, gn, tq):
    heads = []
    for g in range(KVH_A):
        for r in range(G_A):
            hh = g * G_A + r
            rows = slice(r * tq, (r + 1) * tq)
            o = (o_c[g][rows] * gate[:, 3 * hh:3 * hh + 1]
                 + o_s[g][rows] * gate[:, 3 * hh + 1:3 * hh + 2]
                 + o_w[g][rows] * gate[:, 3 * hh + 2:3 * hh + 3])
            heads.append(o)
    return _rms(jnp.concatenate(heads, axis=1), gn)


def _nsa_prompt_kernel(tq, tk, q_ref, ga_ref, cc_ref, sk_ref, sv_ref, wk_ref, wv_ref, gn_ref, out_ref):
    i = pl.program_id(1)
    q0 = i * tq
    n_s = cc_ref.shape[3]
    seq = sk_ref.shape[1]
    n_top = min(N_SEL, n_s)
    scale = HD_A ** -0.5
    q = q_ref[...] * scale
    qpos = q0 + _iota((tq, 1), 0)
    blk = _iota((tq, n_s), 1)
    m_e = blk * SEL_BLOCK + (CMP_BLOCK - 1) <= qpos
    m_o = blk * SEL_BLOCK + (2 * CMP_BLOCK - 1) <= qpos
    n_kt = (q0 + tq - 1) // tk + 1
    w_len = WINDOW + tq
    w_start = pl.multiple_of(jnp.maximum(q0 - WINDOW, 0), tq)
    wpos = w_start + _iota((tq, w_len), 1)
    rel = qpos - wpos
    m_w = (rel >= 0) & (rel < WINDOW)
    o_c, o_s, o_w = [], [], []
    for g in range(KVH_A):
        qg = jnp.concatenate([q[:, (g * G_A + r) * HD_A:(g * G_A + r + 1) * HD_A] for r in range(G_A)],
                             axis=0).astype(BF16)
        oc, imp = _cmp_branch(qg, cc_ref[0, 0, g], cc_ref[0, 1, g], cc_ref[0, 0, KVH_A + g],
                              cc_ref[0, 1, KVH_A + g], m_e, m_o, G_A)
        o_c.append(oc)
        score = _block_scores(imp, qpos, blk)
        rank = jnp.zeros((tq, n_s), jnp.int32)
        for j in range(n_s):
            col = score[:, j:j + 1]
            rank = rank + ((col > score) | ((col == score) & (blk > j))).astype(jnp.int32)
        sel = (rank < n_top).astype(BF16)

        def body(kt, carry):
            m_run, l_run, acc = carry
            k0 = pl.multiple_of(kt * tk, tk)
            k = sk_ref[0, pl.ds(k0, tk), g * HD_A:(g + 1) * HD_A]
            v = sv_ref[0, pl.ds(k0, tk), g * HD_A:(g + 1) * HD_A]
            kpos = k0 + _iota((n_s, tk), 1)
            expand = (kpos // SEL_BLOCK == _iota((n_s, tk), 0)).astype(BF16)
            keep = (jnp.dot(sel, expand, preferred_element_type=F32) > 0.5) & (k0 + _iota((tq, tk), 1) <= qpos)
            s = jnp.where(keep[None], _dot_nt(qg, k).reshape(G_A, tq, tk), NEG)
            m_new = jnp.maximum(m_run, jnp.max(s, axis=-1, keepdims=True))
            alpha = jnp.exp(m_run - m_new)
            e = jnp.exp(s - m_new)
            l_new = alpha * l_run + jnp.sum(e, axis=-1, keepdims=True)
            acc = alpha.reshape(G_A * tq, 1) * acc + _dot(e.reshape(G_A * tq, tk), v)
            return m_new, l_new, acc

        m_fin, l_fin, acc = lax.fori_loop(
            0, n_kt, body,
            (jnp.full((G_A, tq, 1), NEG, F32), jnp.zeros((G_A, tq, 1), F32), jnp.zeros((G_A * tq, HD_A), F32)))
        o_s.append(acc / l_fin.reshape(G_A * tq, 1))

        kw = wk_ref[0, pl.ds(w_start, w_len), g * HD_A:(g + 1) * HD_A]
        vw = wv_ref[0, pl.ds(w_start, w_len), g * HD_A:(g + 1) * HD_A]
        s = jnp.where(m_w[None], _dot_nt(qg, kw).reshape(G_A, tq, w_len), NEG)
        e = jnp.exp(s - jnp.max(s, axis=-1, keepdims=True))
        p = jnp.where(m_w[None], e / jnp.sum(e, axis=-1, keepdims=True), 0.0)
        o_w.append(_dot(p.reshape(G_A * tq, w_len), vw))
    gate = jax.nn.sigmoid(ga_ref[...])
    out_ref[...] = _gate_mix_norm(o_c, o_s, o_w, gate, gn_ref[...], tq)


def _nsa_prompt(qa, ga, cc, nsa3, win3, gn, tq, tk):
    n = qa.shape[0]
    bsz, seq, _ = nsa3.shape
    per_seq = seq // tq
    n_s = cc.shape[3]
    kvw = KVH_A * HD_A
    return pl.pallas_call(
        functools.partial(_nsa_prompt_kernel, tq, tk),
        grid=(bsz, per_seq),
        in_specs=[pl.BlockSpec((tq, W_A), lambda b, i: (b * per_seq + i, 0)),
                  pl.BlockSpec((tq, GATE_PAD), lambda b, i: (b * per_seq + i, 0)),
                  pl.BlockSpec((1, 2, 2 * KVH_A, n_s, HD_A), lambda b, i: (b, 0, 0, 0, 0)),
                  pl.BlockSpec((1, seq, kvw), lambda b, i: (b, 0, 2)),
                  pl.BlockSpec((1, seq, kvw), lambda b, i: (b, 0, 3)),
                  pl.BlockSpec((1, seq, kvw), lambda b, i: (b, 0, 0)),
                  pl.BlockSpec((1, seq, kvw), lambda b, i: (b, 0, 1)),
                  pl.BlockSpec((1, W_A), lambda b, i: (0, 0))],
        out_specs=pl.BlockSpec((tq, W_A), lambda b, i: (b * per_seq + i, 0)),
        out_shape=jax.ShapeDtypeStruct((n, W_A), F32),
        compiler_params=_cparams(2),
        name="nsa_prompt",
    )(qa, ga, cc, nsa3, nsa3, win3, win3, gn)


def _lru_gates(xc, wa_ref, ba_ref, wx_ref, bx_ref, lam_ref):
    r = jax.nn.sigmoid(_dot(xc, wa_ref[...]) + ba_ref[...])
    ig = jax.nn.sigmoid(_dot(xc, wx_ref[...]) + bx_ref[...])
    log_a = -LRU_C * r * jax.nn.softplus(-lam_ref[...])
    a = jnp.exp(log_a)
    u = jnp.sqrt(jnp.tanh(-log_a) * (a * a + 1.0)) * (ig * xc)
    return a, u


def _lru_prompt_kernel(tl, xb_ref, yb_ref, cw_ref, cb_ref, wa_ref, ba_ref, wx_ref, bx_ref, lam_ref, gn_ref,
                       ob_ref, h_ref, cs_ref, xpad, h_carry):
    i = pl.program_id(1)

    @pl.when(i == 0)
    def _():
        xpad[0:SUBLANES, :] = jnp.zeros((SUBLANES, W_B), F32)
        h_carry[...] = jnp.zeros((1, W_B), F32)

    x = xb_ref[...]
    xpad[SUBLANES:SUBLANES + tl, :] = x
    xc = cb_ref[...]
    for k in range(LRU_CONV):
        off = SUBLANES - (LRU_CONV - 1) + k
        xc = xc + xpad[off:off + tl, :] * cw_ref[k:k + 1, :]
    a, u = _lru_gates(xc, wa_ref, ba_ref, wx_ref, bx_ref, lam_ref)
    row = _iota((tl, W_B), 0)
    s = 1
    while s < tl:
        a_sh = jnp.where(row >= s, pltpu.roll(a, s, 0), 1.0)
        u_sh = jnp.where(row >= s, pltpu.roll(u, s, 0), 0.0)
        u = a * u_sh + u
        a = a * a_sh
        s *= 2
    h = a * h_carry[...] + u
    h_carry[...] = h[tl - 1:tl, :]
    xpad[0:SUBLANES, :] = x[tl - SUBLANES:tl, :]
    ob_ref[...] = _rms(_gelu(yb_ref[...]) * h, gn_ref[...])
    h_ref[0] = h[tl - 1:tl, :]
    cs_ref[0] = x[tl - (LRU_CONV - 1):tl, :]


def _lru_prompt(xb, yb, bsz, seq, cw, cb, wa, ba, wx, bx, lam, gn, tl):
    n = xb.shape[0]
    per_seq = seq // tl
    row = lambda b, i: (b * per_seq + i, 0)
    const = lambda b, i: (0, 0)
    vec = pl.BlockSpec((1, W_B), const)
    return pl.pallas_call(
        functools.partial(_lru_prompt_kernel, tl),
        grid=(bsz, per_seq),
        in_specs=[pl.BlockSpec((tl, W_B), row), pl.BlockSpec((tl, W_B), row),
                  pl.BlockSpec((LRU_CONV, W_B), const), vec,
                  pl.BlockSpec((W_B, W_B), const), vec, pl.BlockSpec((W_B, W_B), const), vec, vec, vec],
        out_specs=(pl.BlockSpec((tl, W_B), row),
                   pl.BlockSpec((1, 1, W_B), lambda b, i: (b, 0, 0)),
                   pl.BlockSpec((1, LRU_CONV - 1, W_B), lambda b, i: (b, 0, 0))),
        out_shape=(jax.ShapeDtypeStruct((n, W_B), F32),
                   jax.ShapeDtypeStruct((bsz, 1, W_B), F32),
                   jax.ShapeDtypeStruct((bsz, LRU_CONV - 1, W_B), F32)),
        scratch_shapes=[pltpu.VMEM((SUBLANES + tl, W_B), F32), pltpu.VMEM((1, W_B), F32)],
        compiler_params=_cparams(2),
        name="lru_prompt",
    )(xb, yb, cw, cb, wa, ba, wx, bx, lam, gn)


def _lru_step_kernel(xb_ref, yb_ref, buf_ref, h0_ref, cw_ref, cb_ref, wa_ref, ba_ref, wx_ref, bx_ref,
                     lam_ref, gn_ref, ob_ref, h_ref):
    x = xb_ref[...]
    xc = cb_ref[...]
    for k in range(LRU_CONV - 1):
        xc = xc + buf_ref[k] * cw_ref[k:k + 1, :]
    xc = xc + x * cw_ref[LRU_CONV - 1:LRU_CONV, :]
    a, u = _lru_gates(xc, wa_ref, ba_ref, wx_ref, bx_ref, lam_ref)
    h = a * h0_ref[...] + u
    ob_ref[...] = _rms(_gelu(yb_ref[...]) * h, gn_ref[...])
    h_ref[...] = h


def _lru_step(xb, yb, buf_t, h0, cw, cb, wa, ba, wx, bx, lam, gn):
    n = xb.shape[0]
    return pl.pallas_call(
        _lru_step_kernel,
        out_shape=(jax.ShapeDtypeStruct((n, W_B), F32), jax.ShapeDtypeStruct((n, W_B), F32)),
        compiler_params=pltpu.CompilerParams(vmem_limit_bytes=VMEM_LIMIT),
        name="lru_step",
    )(xb, yb, buf_t, h0, cw, cb, wa, ba, wx, bx, lam, gn)


def _diff_lambda(dl_ref, lam_init):
    dl = dl_ref[...]
    s01 = jnp.sum(dl[0:1, :] * dl[1:2, :], axis=-1, keepdims=True)
    s23 = jnp.sum(dl[2:3, :] * dl[3:4, :], axis=-1, keepdims=True)
    return jnp.exp(s01) - jnp.exp(s23) + lam_init


def _diff_prompt_kernel(tq, tk, lam_init, q_ref, k_ref, v_ref, dl_ref, sub_ref, out_ref):
    i = pl.program_id(1)
    q0 = i * tq
    lam = _diff_lambda(dl_ref, lam_init)
    n_kt = (q0 + tq - 1) // tk + 1
    qpos = q0 + _iota((tq, 1), 0)
    q = q_ref[...] * (DC ** -0.5)
    outs = []
    for h in range(N_C):
        res = []
        for c in range(2):
            lo = h * DV_C + c * DC
            qc = q[:, lo:lo + DC].astype(BF16)

            def body(kt, carry):
                m_run, l_run, acc = carry
                k0 = pl.multiple_of(kt * tk, tk)
                k = k_ref[0, pl.ds(k0, tk), lo:lo + DC]
                v = v_ref[0, pl.ds(k0, tk), h * DV_C:(h + 1) * DV_C]
                keep = k0 + _iota((tq, tk), 1) <= qpos
                s = jnp.where(keep, _dot_nt(qc, k), NEG)
                m_new = jnp.maximum(m_run, jnp.max(s, axis=-1, keepdims=True))
                alpha = jnp.exp(m_run - m_new)
                e = jnp.exp(s - m_new)
                return (m_new, alpha * l_run + jnp.sum(e, axis=-1, keepdims=True),
                        alpha * acc + _dot(e, v))

            _, l_fin, acc = lax.fori_loop(
                0, n_kt, body,
                (jnp.full((tq, 1), NEG, F32), jnp.zeros((tq, 1), F32), jnp.zeros((tq, DV_C), F32)))
            res.append(acc / l_fin)
        od = res[0] - lam * res[1]
        outs.append(_rms(od, sub_ref[...]) * (1.0 - lam_init))
    out_ref[...] = jnp.concatenate(outs, axis=1)


def _diff_prompt(qc, kc3, vc3, dl, sub, lam_init, tq, tk):
    n = qc.shape[0]
    bsz, seq, _ = kc3.shape
    per_seq = seq // tq
    return pl.pallas_call(
        functools.partial(_diff_prompt_kernel, tq, tk, lam_init),
        grid=(bsz, per_seq),
        in_specs=[pl.BlockSpec((tq, W_C), lambda b, i: (b * per_seq + i, 0)),
                  pl.BlockSpec((1, seq, W_C), lambda b, i: (b, 0, 0)),
                  pl.BlockSpec((1, seq, W_C), lambda b, i: (b, 0, 0)),
                  pl.BlockSpec((4, DC), lambda b, i: (0, 0)),
                  pl.BlockSpec((1, DV_C), lambda b, i: (0, 0))],
        out_specs=pl.BlockSpec((tq, W_C), lambda b, i: (b * per_seq + i, 0)),
        out_shape=jax.ShapeDtypeStruct((n, W_C), F32),
        compiler_params=_cparams(2),
        name="diff_prompt",
    )(qc, kc3, vc3, dl, sub)


def _diff_paged_kernel(kp, lam_init, pt_ref, q_ref, kn_ref, vn_ref, dl_ref, sub_ref, *rest):
    k_refs, v_refs = rest[:kp], rest[kp:2 * kp]
    out_ref, m_sc, l_sc, acc_sc = rest[2 * kp:]
    c = pl.program_id(1)
    n_rows = 2 * N_C

    @pl.when(c == 0)
    def _():
        m_sc[...] = jnp.full((n_rows, 1), NEG, F32)
        l_sc[...] = jnp.zeros((n_rows, 1), F32)
        acc_sc[...] = jnp.zeros((n_rows, W_C), F32)

    own = _iota((n_rows, W_C), 1) // DC == _iota((n_rows, W_C), 0)
    q_bd = jnp.where(own, jnp.broadcast_to(q_ref[0], (n_rows, W_C)), 0.0) * (DC ** -0.5)
    q_bf = q_bd.astype(BF16)
    s = jnp.concatenate([_dot_nt(q_bf, k_refs[j][0]) for j in range(kp)], axis=1)
    m_run = m_sc[...]
    m_new = jnp.maximum(m_run, jnp.max(s, axis=-1, keepdims=True))
    alpha = jnp.exp(m_run - m_new)
    e = jnp.exp(s - m_new)
    page = k_refs[0].shape[1]
    acc = alpha * acc_sc[...]
    for j in range(kp):
        acc = acc + _dot(e[:, j * page:(j + 1) * page], v_refs[j][0])
    l_new = alpha * l_sc[...] + jnp.sum(e, axis=-1, keepdims=True)
    m_sc[...] = m_new
    l_sc[...] = l_new
    acc_sc[...] = acc

    @pl.when(c == pl.num_programs(1) - 1)
    def _():
        s_n = jnp.sum(q_bd * kn_ref[0], axis=-1, keepdims=True)
        m_f = jnp.maximum(m_new, s_n)
        al = jnp.exp(m_new - m_f)
        e_n = jnp.exp(s_n - m_f)
        o = (al * acc + e_n * vn_ref[0]) / (al * l_new + e_n)
        lam = _diff_lambda(dl_ref, lam_init)
        lane_head = _iota((n_rows, W_C), 1) // DV_C
        rowi = _iota((n_rows, W_C), 0)
        comb = jnp.where(rowi == 2 * lane_head, o, 0.0) - lam * jnp.where(rowi == 2 * lane_head + 1, o, 0.0)
        od = jnp.sum(comb, axis=0, keepdims=True)
        heads = [_rms(od[:, h * DV_C:(h + 1) * DV_C], sub_ref[...]) * (1.0 - lam_init) for h in range(N_C)]
        out_ref[0] = jnp.concatenate(heads, axis=1)


def _diff_paged(qc, kc_new, vc_new, cache_k4, cache_v4, layer, page_table_flat, n_pages, dl, sub, lam_init, kp):
    n_seq = qc.shape[0]
    page = cache_k4.shape[2]

    def page_spec(j):
        return pl.BlockSpec((None, 1, page, W_C),
                            lambda b, c, pt: (layer, pt[b * n_pages + c * kp + j], 0, 0))

    row = pl.BlockSpec((1, 1, W_C), lambda b, c, pt: (b, 0, 0))
    grid_spec = pltpu.PrefetchScalarGridSpec(
        num_scalar_prefetch=1,
        grid=(n_seq, n_pages // kp),
        in_specs=[row, row, row,
                  pl.BlockSpec((4, DC), lambda b, c, pt: (0, 0)),
                  pl.BlockSpec((1, DV_C), lambda b, c, pt: (0, 0))]
                 + [page_spec(j) for j in range(kp)] + [page_spec(j) for j in range(kp)],
        out_specs=row,
        scratch_shapes=[pltpu.VMEM((2 * N_C, 1), F32), pltpu.VMEM((2 * N_C, 1), F32),
                        pltpu.VMEM((2 * N_C, W_C), F32)],
    )
    return pl.pallas_call(
        functools.partial(_diff_paged_kernel, kp, lam_init),
        grid_spec=grid_spec,
        out_shape=jax.ShapeDtypeStruct((n_seq, 1, W_C), F32),
        compiler_params=_cparams(2),
        name="diff_paged",
    )(page_table_flat, qc, kc_new, vc_new, dl, sub, *([cache_k4] * kp), *([cache_v4] * kp))


def _nsa_step_a_kernel(qpos, n_s, q_ref, cc_ref, wb_ref, wn_ref, oc_ref, ow_ref, idx_ref):
    nsp = cc_ref.shape[3]
    n_buf = wb_ref.shape[1]
    q = q_ref[0] * (HD_A ** -0.5)
    blk = _iota((1, nsp), 1)
    real = blk < n_s
    m_e = real & (blk * SEL_BLOCK + (CMP_BLOCK - 1) <= qpos)
    m_o = real & (blk * SEL_BLOCK + (2 * CMP_BLOCK - 1) <= qpos)
    kpos = qpos - n_buf + _iota((1, n_buf), 1)
    rel = qpos - kpos
    m_w = (rel >= 0) & (rel < WINDOW) & (kpos >= 0)
    rows = _iota((nsp, nsp), 0)
    cols = _iota((nsp, nsp), 1)
    lane = _iota((1, LANES), 1)
    oc_parts, ow_parts, idx_rows = [], [], []
    for g in range(KVH_A):
        qg = jnp.concatenate([q[:, (g * G_A + r) * HD_A:(g * G_A + r + 1) * HD_A] for r in range(G_A)], axis=0)
        oc, imp = _cmp_branch(qg, cc_ref[0, 0, g], cc_ref[0, 1, g], cc_ref[0, 0, KVH_A + g],
                              cc_ref[0, 1, KVH_A + g], m_e, m_o, G_A)
        oc_parts.append(oc)
        score = jnp.where(real, _block_scores(imp, qpos, blk), -jnp.inf)
        s_row = jnp.broadcast_to(score, (nsp, nsp))
        s_col = jnp.transpose(s_row)
        beats = (s_col > s_row) | ((s_col == s_row) & (rows < cols))
        rank = jnp.sum(beats.astype(jnp.int32), axis=0, keepdims=True)
        idx = jnp.zeros((1, LANES), jnp.int32)
        blk_f = blk.astype(F32)
        for r in range(min(N_SEL, n_s)):
            pick = jnp.sum(jnp.where(rank == r, blk_f, 0.0), axis=-1, keepdims=True).astype(jnp.int32)
            idx = jnp.where(lane == r, pick, idx)
        idx_rows.append(idx)
        kb = wb_ref[0, :, g * HD_A:(g + 1) * HD_A]
        vb = wb_ref[0, :, (KVH_A + g) * HD_A:(KVH_A + g + 1) * HD_A]
        kn = wn_ref[0, :, g * HD_A:(g + 1) * HD_A]
        vn = wn_ref[0, :, (KVH_A + g) * HD_A:(KVH_A + g + 1) * HD_A]
        s_b = jnp.where(m_w, _dot_nt(qg, kb), NEG)
        s_n = jnp.sum(qg.astype(BF16).astype(F32) * kn.astype(BF16).astype(F32), axis=-1, keepdims=True)
        mx = jnp.maximum(jnp.max(s_b, axis=-1, keepdims=True), s_n)
        e_b = jnp.exp(s_b - mx)
        e_n = jnp.exp(s_n - mx)
        den = jnp.sum(e_b, axis=-1, keepdims=True) + e_n
        p_b = jnp.where(m_w, e_b / den, 0.0)
        ow_parts.append(_dot(p_b, vb) + (e_n / den) * vn)
    oc_ref[0] = jnp.concatenate(oc_parts, axis=0)
    ow_ref[0] = jnp.concatenate(ow_parts, axis=0)
    idx_ref[0] = jnp.concatenate(idx_rows + [jnp.zeros((SUBLANES - KVH_A, LANES), jnp.int32)], axis=0)


def _nsa_step_a(qa3, cc, win_cache4, layer, win_new3, qpos, n_s):
    n_seq = qa3.shape[0]
    nsp = cc.shape[3]
    n_buf = win_cache4.shape[2]
    kvw = 2 * KVH_A * HD_A
    return pl.pallas_call(
        functools.partial(_nsa_step_a_kernel, qpos, n_s),
        grid=(n_seq,),
        in_specs=[pl.BlockSpec((1, 1, W_A), lambda b: (b, 0, 0)),
                  pl.BlockSpec((1, 2, 2 * KVH_A, nsp, HD_A), lambda b: (b, 0, 0, 0, 0)),
                  pl.BlockSpec((None, 1, n_buf, kvw), lambda b: (layer, b, 0, 0)),
                  pl.BlockSpec((1, 1, kvw), lambda b: (b, 0, 0))],
        out_specs=(pl.BlockSpec((1, H_A, HD_A), lambda b: (b, 0, 0)),
                   pl.BlockSpec((1, H_A, HD_A), lambda b: (b, 0, 0)),
                   pl.BlockSpec((1, SUBLANES, LANES), lambda b: (b, 0, 0))),
        out_shape=(jax.ShapeDtypeStruct((n_seq, H_A, HD_A), F32),
                   jax.ShapeDtypeStruct((n_seq, H_A, HD_A), F32),
                   jax.ShapeDtypeStruct((n_seq, SUBLANES, LANES), jnp.int32)),
        compiler_params=_cparams(1),
        name="nsa_step_rank",
    )(qa3, cc, win_cache4, win_new3)


def _nsa_step_b_kernel(n_top, n_past, idx_ref, pt_ref, q_ref, oc_ref, ow_ref, ga_ref, new_ref, gn_ref, *rest):
    nk = KVH_A * n_top
    k_refs, v_refs = rest[:nk], rest[nk:2 * nk]
    out_ref = rest[2 * nk]
    b = pl.program_id(0)
    q = q_ref[0] * (HD_A ** -0.5)
    o_s = []
    for g in range(KVH_A):
        qg = jnp.concatenate([q[:, (g * G_A + r) * HD_A:(g * G_A + r + 1) * HD_A] for r in range(G_A)], axis=0)
        lanes = slice(g * HD_A, (g + 1) * HD_A)
        s_parts = []
        has_tail = False
        for j in range(n_top):
            blk = idx_ref[(b * KVH_A + g) * n_top + j]
            s_j = _dot_nt(qg, k_refs[g * n_top + j][0, :, lanes])
            s_parts.append(jnp.where(blk < n_past, s_j, NEG))
            has_tail = (blk == n_past) | has_tail
        s = jnp.concatenate(s_parts, axis=1)
        kn = new_ref[0, :, 2 * KVH_A * HD_A + g * HD_A:2 * KVH_A * HD_A + (g + 1) * HD_A]
        vn = new_ref[0, :, 3 * KVH_A * HD_A + g * HD_A:3 * KVH_A * HD_A + (g + 1) * HD_A]
        s_n = jnp.sum(qg.astype(BF16).astype(F32) * kn.astype(BF16).astype(F32), axis=-1, keepdims=True)
        s_n = jnp.where(has_tail, s_n, NEG)
        mx = jnp.maximum(jnp.max(s, axis=-1, keepdims=True), s_n)
        e = jnp.exp(s - mx)
        e_n = jnp.exp(s_n - mx)
        den = jnp.sum(e, axis=-1, keepdims=True) + e_n
        e = jnp.where(s > 0.5 * NEG, e, 0.0)
        e_n = jnp.where(s_n > 0.5 * NEG, e_n, 0.0)
        acc = (e_n / den) * vn
        for j in range(n_top):
            acc = acc + _dot(e[:, j * SEL_BLOCK:(j + 1) * SEL_BLOCK] / den, v_refs[g * n_top + j][0, :, lanes])
        o_s.append(acc)
    oc = oc_ref[0]
    ow = ow_ref[0]
    gate = jax.nn.sigmoid(ga_ref[0])
    out_ref[0] = _gate_mix_norm([oc[0:G_A], oc[G_A:2 * G_A]], o_s, [ow[0:G_A], ow[G_A:2 * G_A]],
                                gate, gn_ref[...], 1)


def _nsa_step_select(idx_flat, page_table_flat, n_pages, blocks_per_page, cache_blk4, layer,
                     qa3, oc, ow, ga3, nsa_new3, gn, n_top):
    n_seq = qa3.shape[0]
    kvw = KVH_A * HD_A
    n_past = n_pages * blocks_per_page

    def blk_spec(g, j, lane_blk):
        def imap(b, idx, pt):
            blk = jnp.minimum(idx[(b * KVH_A + g) * n_top + j], n_past - 1)
            page = pt[b * n_pages + blk // blocks_per_page]
            return (layer, page * blocks_per_page + blk % blocks_per_page, 0, lane_blk)
        return pl.BlockSpec((None, 1, SEL_BLOCK, kvw), imap)

    def row(w):
        return pl.BlockSpec((1, 1, w), lambda b, idx, pt: (b, 0, 0))

    head = pl.BlockSpec((1, H_A, HD_A), lambda b, idx, pt: (b, 0, 0))
    pairs = [(g, j) for g in range(KVH_A) for j in range(n_top)]
    grid_spec = pltpu.PrefetchScalarGridSpec(
        num_scalar_prefetch=2,
        grid=(n_seq,),
        in_specs=[row(W_A), head, head, row(GATE_PAD), row(4 * kvw),
                  pl.BlockSpec((1, W_A), lambda b, idx, pt: (0, 0))]
                 + [blk_spec(g, j, 2) for g, j in pairs] + [blk_spec(g, j, 3) for g, j in pairs],
        out_specs=row(W_A),
    )
    n_in = 2 * len(pairs)
    return pl.pallas_call(
        functools.partial(_nsa_step_b_kernel, n_top, n_past),
        grid_spec=grid_spec,
        out_shape=jax.ShapeDtypeStruct((n_seq, 1, W_A), F32),
        compiler_params=_cparams(1),
        name="nsa_step_select",
    )(idx_flat, page_table_flat, qa3, oc, ow, ga3, nsa_new3, gn, *([cache_blk4] * n_in))


def _ffn_chunks():
    fc = 256
    assert D_FF % fc == 0
    return fc, D_FF // fc


def _ffn_tail(x1, nf_ref, wup_ref, wdn_ref, conv_fn, final_gain, act_sc):
    fc, n_fc = _ffn_chunks()
    h2 = _rms(x1, nf_ref[...]).astype(BF16)
    for c in range(n_fc):
        g = jnp.dot(h2, wup_ref[:, c * fc:(c + 1) * fc], preferred_element_type=F32)
        v = jnp.dot(h2, wup_ref[:, D_FF + c * fc:D_FF + (c + 1) * fc], preferred_element_type=F32)
        act_sc[:, c * fc:(c + 1) * fc] = (_gelu(conv_fn(c, g)) * v).astype(BF16)
    y = x1 + jnp.dot(act_sc[...], wdn_ref[...], preferred_element_type=F32)
    if final_gain is not None:
        y = _rms(y, final_gain[...])
    return y


def _mix_residual(x_ref, oa_ref, ob_ref, oc_ref, wo_ref, mix_sc):
    mix_sc[:, 0:W_A] = oa_ref[...].astype(BF16)
    mix_sc[:, W_A:W_A + W_B] = ob_ref[...].astype(BF16)
    mix_sc[:, W_A + W_B:] = oc_ref[...].astype(BF16)
    return x_ref[...] + jnp.dot(mix_sc[...], wo_ref[...], preferred_element_type=F32)


def _ffn_prompt_kernel(tm, last, x_ref, oa_ref, ob_ref, oc_ref, wo_ref, nf_ref, wup_ref, cw_ref, cb_ref,
                       wdn_ref, fin_ref, y_ref, st_ref, gpad, carry, mix_sc, act_sc):
    i = pl.program_id(1)
    fc, _ = _ffn_chunks()

    @pl.when(i == 0)
    def _():
        carry[...] = jnp.zeros(carry.shape, F32)

    def conv(c, g):
        cols = slice(c * fc, (c + 1) * fc)
        gpad[0:SUBLANES, :] = carry[:, cols]
        gpad[SUBLANES:SUBLANES + tm, :] = g
        out = cb_ref[:, cols]
        for k in range(FFN_CONV):
            off = SUBLANES - (FFN_CONV - 1) + k
            out = out + gpad[off:off + tm, :] * cw_ref[k:k + 1, cols]
        carry[:, cols] = g[tm - SUBLANES:tm, :]
        return out

    x1 = _mix_residual(x_ref, oa_ref, ob_ref, oc_ref, wo_ref, mix_sc)
    y_ref[...] = _ffn_tail(x1, nf_ref, wup_ref, wdn_ref, conv, fin_ref if last else None, act_sc)
    st_ref[0] = carry[SUBLANES - (FFN_CONV - 1):SUBLANES, :]


def _ffn_prompt(x2d, oa, ob, oc, wo, nf, wup, cw, cb, wdn, fin, bsz, seq, tm, last):
    n = x2d.shape[0]
    per_seq = seq // tm
    fc, _ = _ffn_chunks()
    row = lambda b, i: (b * per_seq + i, 0)
    const = lambda b, i: (0, 0)
    one = pl.Buffered(1)
    return pl.pallas_call(
        functools.partial(_ffn_prompt_kernel, tm, last),
        grid=(bsz, per_seq),
        in_specs=[pl.BlockSpec((tm, D_MODEL), row), pl.BlockSpec((tm, W_A), row),
                  pl.BlockSpec((tm, W_B), row), pl.BlockSpec((tm, W_C), row),
                  pl.BlockSpec((D_MODEL, D_MODEL), const, pipeline_mode=one),
                  pl.BlockSpec((1, D_MODEL), const),
                  pl.BlockSpec((D_MODEL, 2 * D_FF), const, pipeline_mode=one),
                  pl.BlockSpec((FFN_CONV, D_FF), const), pl.BlockSpec((1, D_FF), const),
                  pl.BlockSpec((D_FF, D_MODEL), const, pipeline_mode=one),
                  pl.BlockSpec((1, D_MODEL), const)],
        out_specs=(pl.BlockSpec((tm, D_MODEL), row),
                   pl.BlockSpec((1, FFN_CONV - 1, D_FF), lambda b, i: (b, 0, 0))),
        out_shape=(jax.ShapeDtypeStruct((n, D_MODEL), F32),
                   jax.ShapeDtypeStruct((bsz, FFN_CONV - 1, D_FF), F32)),
        scratch_shapes=[pltpu.VMEM((SUBLANES + tm, fc), F32), pltpu.VMEM((SUBLANES, D_FF), F32),
                        pltpu.VMEM((tm, D_MODEL), BF16), pltpu.VMEM((tm, D_FF), BF16)],
        compiler_params=_cparams(2),
        name="ffn_prompt",
    )(x2d, oa, ob, oc, wo, nf, wup, cw, cb, wdn, fin)


def _ffn_step_kernel(last, x_ref, oa_ref, ob_ref, oc_ref, wo_ref, nf_ref, wup_ref, cw_ref, cb_ref, wdn_ref,
                     fin_ref, buf_ref, y_ref, g_ref, mix_sc, act_sc):
    fc, _ = _ffn_chunks()

    def conv(c, g):
        cols = slice(c * fc, (c + 1) * fc)
        g_ref[:, cols] = g
        out = cb_ref[:, cols]
        for k in range(FFN_CONV - 1):
            out = out + buf_ref[k, :, cols] * cw_ref[k:k + 1, cols]
        return out + g * cw_ref[FFN_CONV - 1:FFN_CONV, cols]

    x1 = _mix_residual(x_ref, oa_ref, ob_ref, oc_ref, wo_ref, mix_sc)
    y_ref[...] = _ffn_tail(x1, nf_ref, wup_ref, wdn_ref, conv, fin_ref if last else None, act_sc)


def _ffn_step(x2d, oa, ob, oc, wo, nf, wup, cw, cb, wdn, fin, buf_t, last):
    n = x2d.shape[0]
    return pl.pallas_call(
        functools.partial(_ffn_step_kernel, last),
        out_shape=(jax.ShapeDtypeStruct((n, D_MODEL), F32), jax.ShapeDtypeStruct((n, D_FF), F32)),
        scratch_shapes=[pltpu.VMEM((n, D_MODEL), BF16), pltpu.VMEM((n, D_FF), BF16)],
        compiler_params=pltpu.CompilerParams(vmem_limit_bytes=VMEM_LIMIT),
        name="ffn_step",
    )(x2d, oa, ob, oc, wo, nf, wup, cw, cb, wdn, fin, buf_t)


def _block_diag(blocks):
    nb, k, n = blocks.shape[-3:]
    eye = jnp.eye(nb, dtype=blocks.dtype)
    out = blocks[..., :, :, None, :] * eye[:, None, :, None]
    return out.reshape(blocks.shape[:-3] + (nb * k, nb * n))


def _layer_weights(l, w_in, cmp_pe, cmp_w1, cmp_w2, lru_gate_a_w, lru_gate_x_w, w_out, w_up, w_down):
    w = w_in[l]
    g0 = W_A + 3 * 2 * KVH_A * HD_A
    w_r = jnp.concatenate([w[:, :g0], w[:, g0 + N_GATE:], w[:, g0:g0 + N_GATE],
                           jnp.zeros((D_MODEL, GATE_PAD - N_GATE), F32)], axis=1).astype(BF16)
    pe = cmp_pe[l]
    pe4 = jnp.concatenate([pe] * KVH_A, axis=-1)
    w1 = cmp_w1[l]
    w1bd = _block_diag(jnp.stack([w1] * KVH_A, axis=2)).astype(BF16)
    w1bd = w1bd.reshape(2, CMP_BLOCK * KVH_A * HD_A, KVH_A * HD_A)
    w2 = cmp_w2[l]
    w2bd = _block_diag(jnp.stack([w2] * KVH_A, axis=1)).astype(BF16)
    wa = _block_diag(lru_gate_a_w[l]).astype(BF16)
    wx = _block_diag(lru_gate_x_w[l]).astype(BF16)
    return dict(w_r=w_r, pe4=pe4, w1bd=w1bd, w2bd=w2bd, wa=wa, wx=wx,
                wo=w_out[l].astype(BF16), wup=w_up[l].astype(BF16), wdn=w_down[l].astype(BF16))


def _split_compressed(c, n_s_pad):
    bsz, n_c, _ = c.shape
    n_s = n_c // 2
    cc = c.reshape(bsz, n_s, 2, 2 * KVH_A, HD_A).transpose(0, 2, 3, 1, 4)
    if n_s_pad > n_s:
        cc = jnp.pad(cc, ((0, 0), (0, 0), (0, 0), (0, n_s_pad - n_s), (0, 0)))
    return cc


def kernel(x_prompt, x_sample, cache_nsa_kv, cache_diff_k, cache_diff_v, cache_win_kv, state_lru_h,
           state_lru_conv, state_ffn_conv, page_table, norm_mix, w_in, cmp_pe, cmp_w1, cmp_w2, gn_nsa,
           lru_conv_w, lru_conv_b, lru_gate_a_w, lru_gate_a_b, lru_gate_x_w, lru_gate_x_b, lru_lambda,
           gn_lru, diff_lambda, diff_subln, w_out, norm_ffn, w_up, ffn_conv_w, ffn_conv_b, w_down, norm_final):
    bsz, seq, _ = x_prompt.shape
    n_seq, dec_len, _ = x_sample.shape
    depth, n_pool, page, _, _, _ = cache_nsa_kv.shape
    n_pages = page_table.shape[1]
    past_len = n_pages * page
    assert dec_len == 1 and seq % SEL_BLOCK == 0 and past_len % SEL_BLOCK == 0
    assert seq >= WINDOW + 128 and cache_win_kv.shape[2] == min(WINDOW, past_len)
    n_tok = bsz * seq
    kvw = KVH_A * HD_A

    cache_tok = cache_nsa_kv.reshape(depth, n_pool, page, 4 * kvw)
    blocks_per_page = page // SEL_BLOCK
    cache_blk = cache_nsa_kv.reshape(depth, n_pool * blocks_per_page, SEL_BLOCK, 4 * kvw)
    cache_dk = cache_diff_k.reshape(depth, n_pool, page, W_C)
    cache_dv = cache_diff_v.reshape(depth, n_pool, page, W_C)
    cache_win = cache_win_kv.reshape(depth, n_seq, cache_win_kv.shape[2], 2 * kvw)
    pt_flat = page_table.reshape(-1)

    tabs_p64 = _rope_tables(jnp.arange(seq), HD_A)
    tabs_p32 = _rope_tables(jnp.arange(seq), DC)
    pos_s = past_len + jnp.arange(1)
    tabs_s64 = _rope_tables(pos_s, HD_A)
    tabs_s32 = _rope_tables(pos_s, DC)

    n_s_dec = (past_len + SEL_BLOCK) // SEL_BLOCK
    n_s_pad = -(-n_s_dec // LANES) * LANES
    n_top_dec = min(N_SEL, n_s_dec)

    xp = x_prompt.reshape(n_tok, D_MODEL)
    xs = x_sample.reshape(n_seq, D_MODEL)
    fin = norm_final.reshape(1, D_MODEL)
    st_p = [[] for _ in range(7)]
    st_s = [[] for _ in range(7)]
    for l in range(depth):
        last = l == depth - 1
        lam_init = 0.8 - 0.6 * math.exp(-0.3 * l)
        lw = _layer_weights(l, w_in, cmp_pe, cmp_w1, cmp_w2, lru_gate_a_w, lru_gate_x_w, w_out, w_up, w_down)
        gain = norm_mix[l].reshape(1, D_MODEL)
        gn = gn_nsa[l].reshape(1, W_A)
        lru_args = (lru_conv_w[l], lru_conv_b[l].reshape(1, W_B), lw["wa"], lru_gate_a_b[l].reshape(1, W_B),
                    lw["wx"], lru_gate_x_b[l].reshape(1, W_B), lru_lambda[l].reshape(1, W_B),
                    gn_lru[l].reshape(1, W_B))
        dl = diff_lambda[l]
        sub = diff_subln[l].reshape(1, DV_C)
        ffn_args = (lw["wo"], norm_ffn[l].reshape(1, D_MODEL), lw["wup"], ffn_conv_w[l],
                    ffn_conv_b[l].reshape(1, D_FF), lw["wdn"], fin)

        qa, nsa, win, xb, yb, qc, kc, vc, ga = _inproj(xp, gain, lw["w_r"], tabs_p64, tabs_p32, seq, 512)
        comp = _compress(nsa, lw["pe4"], lw["w1bd"], lw["w2bd"], math.gcd(n_tok, 4096))
        cc = _split_compressed(comp.reshape(bsz, seq // CMP_BLOCK, 4 * HD_A), seq // SEL_BLOCK)
        nsa3 = nsa.reshape(bsz, seq, 4 * kvw)
        win3 = win.reshape(bsz, seq, 2 * kvw)
        oa = _nsa_prompt(qa, ga, cc, nsa3, win3, gn, 128, 256)
        ob, h_last, conv_st = _lru_prompt(xb, yb, bsz, seq, *lru_args, 256)
        kc3 = kc.reshape(bsz, seq, W_C)
        vc3 = vc.reshape(bsz, seq, W_C)
        oc = _diff_prompt(qc, kc3, vc3, dl, sub, lam_init, 256, 512)
        xp, ffn_st = _ffn_prompt(xp, oa, ob, oc, *ffn_args, bsz, seq, 256, last)
        n_win = min(WINDOW, seq)
        st_p[0].append(nsa.reshape(bsz, seq, 4, KVH_A, HD_A))
        st_p[1].append(win3[:, seq - n_win:].reshape(bsz, n_win, 2, KVH_A, HD_A))
        st_p[2].append(kc.reshape(bsz, seq, N_C, 2, DC))
        st_p[3].append(vc.reshape(bsz, seq, N_C, DV_C))
        st_p[4].append(h_last.reshape(bsz, W_B))
        st_p[5].append(conv_st)
        st_p[6].append(ffn_st)

        qa, nsa, win, xb, yb, qc, kc, vc, ga = _inproj(xs, gain, lw["w_r"], tabs_s64, tabs_s32, 1, n_seq)
        comp_past = _compress_paged(cache_tok, l, pt_flat, n_seq, n_pages, lw["pe4"], lw["w1bd"], lw["w2bd"],
                                    math.gcd(n_pages, 32))
        tail = jnp.pad(nsa.reshape(n_seq, 1, 4 * kvw), ((0, 0), (0, SEL_BLOCK - 1), (0, 0)))
        comp_tail = _compress(tail.reshape(n_seq * SEL_BLOCK, 4 * kvw), lw["pe4"], lw["w1bd"], lw["w2bd"],
                              n_seq * SEL_BLOCK)
        comp = jnp.concatenate([comp_past, comp_tail.reshape(n_seq, SEL_BLOCK // CMP_BLOCK, 4 * HD_A)], axis=1)
        cc = _split_compressed(comp, n_s_pad)
        qa3 = qa.reshape(n_seq, 1, W_A)
        o_cmp, o_win, idx = _nsa_step_a(qa3, cc, cache_win, l, win.reshape(n_seq, 1, 2 * kvw), past_len, n_s_dec)
        idx_flat = idx[:, :KVH_A, :n_top_dec].reshape(-1)
        oa = _nsa_step_select(idx_flat, pt_flat, n_pages, blocks_per_page, cache_blk, l, qa3, o_cmp, o_win,
                              ga.reshape(n_seq, 1, GATE_PAD), nsa.reshape(n_seq, 1, 4 * kvw), gn,
                              n_top_dec).reshape(n_seq, W_A)
        buf_t = jnp.transpose(state_lru_conv[l], (1, 0, 2))
        ob, h_new = _lru_step(xb, yb, buf_t, state_lru_h[l], *lru_args)
        oc = _diff_paged(qc.reshape(n_seq, 1, W_C), kc.reshape(n_seq, 1, W_C), vc.reshape(n_seq, 1, W_C),
                         cache_dk, cache_dv, l, pt_flat, n_pages, dl, sub, lam_init,
                         math.gcd(n_pages, 16)).reshape(n_seq, W_C)
        fbuf_t = jnp.transpose(state_ffn_conv[l], (1, 0, 2))
        xs, g_new = _ffn_step(xs, oa, ob, oc, *ffn_args, fbuf_t, last)
        st_s[0].append(nsa.reshape(n_seq, 1, 4, KVH_A, HD_A))
        st_s[1].append(win.reshape(n_seq, 1, 2, KVH_A, HD_A))
        st_s[2].append(kc.reshape(n_seq, 1, N_C, 2, DC))
        st_s[3].append(vc.reshape(n_seq, 1, N_C, DV_C))
        st_s[4].append(h_new)
        st_s[5].append(jnp.concatenate([state_lru_conv[l][:, 1:], xb[:, None, :]], axis=1))
        st_s[6].append(jnp.concatenate([state_ffn_conv[l][:, 1:], g_new[:, None, :]], axis=1))

    y_prompt = xp.reshape(bsz, seq, D_MODEL)
    y_sample = xs.reshape(n_seq, 1, D_MODEL)
    return (y_prompt, y_sample) + tuple(jnp.stack(s, axis=0) for s in st_p) + tuple(
        jnp.stack(s, axis=0) for s in st_s)
```

```python
import functools
import math

import jax
import jax.numpy as jnp
from jax import lax
from jax.experimental import pallas as pl
from jax.experimental.pallas import tpu as pltpu

F32 = jnp.float32
BF16 = jnp.bfloat16

D_MODEL = 1024
HD_A = 64
W_A = D_MODEL // 2
H_A = W_A // HD_A
KVH_A = 2
G_A = H_A // KVH_A
CMP_BLOCK = 32
SEL_BLOCK = 64
N_SEL = 16
WINDOW = 512
FORCE_BONUS = 1.0e4
W_B = D_MODEL // 4
LRU_BLOCKS = 4
LRU_CONV = 4
LRU_C = 8.0
W_C = D_MODEL // 4
N_C = 4
DV_C = W_C // N_C
DC = DV_C // 2
D_FF = ((8 * D_MODEL // 3 + 127) // 128) * 128
FFN_CONV = 3
ROPE_THETA = 10000.0
EPS = 1e-6
NEG = -1e30
LOG2E = 1.4426950408889634

LANES = 128
SUBLANES = 8
N_GATE = 3 * H_A
GATE_PAD = LANES
N_Z = W_A + 3 * 2 * KVH_A * HD_A + 5 * W_B + GATE_PAD
VMEM_LIMIT = 56 * 1024 * 1024


def _cparams(n_axes):
    return pltpu.CompilerParams(dimension_semantics=("arbitrary",) * n_axes,
                                vmem_limit_bytes=VMEM_LIMIT)


def _rms(x, g):
    return x * lax.rsqrt(jnp.mean(x * x, axis=-1, keepdims=True) + EPS) * g


def _dot(a, b):
    return jnp.dot(a.astype(BF16), b.astype(BF16), preferred_element_type=F32)


def _dot_nt(a, b):
    return lax.dot_general(a.astype(BF16), b.astype(BF16), (((1,), (1,)), ((), ())),
                           preferred_element_type=F32)


def _iota(shape, dim):
    return lax.broadcasted_iota(jnp.int32, shape, dim)


def _gelu(x):
    return jax.nn.gelu(x)


def _rope_tables(pos, d):
    inv = ROPE_THETA ** (-jnp.arange(0, d, 2, dtype=F32) / d)
    ang = pos.astype(F32)[:, None] * inv[None, :]
    cos, sin = jnp.cos(ang), jnp.sin(ang)
    zero = jnp.zeros_like(sin)
    rep = LANES // d
    cos_t = jnp.tile(jnp.concatenate([cos, cos], axis=1), (1, rep))
    sin_lo = jnp.tile(jnp.concatenate([-sin, zero], axis=1), (1, rep))
    sin_hi = jnp.tile(jnp.concatenate([zero, sin], axis=1), (1, rep))
    return cos_t, sin_lo, sin_hi


def _inproj_kernel(x_ref, g_ref, w_ref, c64, sl64, sh64, c32, sl32, sh32,
                   qa_ref, nsa_ref, win_ref, xb_ref, yb_ref, qc_ref, kc_ref, vc_ref, ga_ref):
    h = _rms(x_ref[...], g_ref[...]).astype(BF16)

    def seg(a, n):
        return jnp.dot(h, w_ref[:, a:a + n], preferred_element_type=F32)

    def rope(z, half, tabs):
        cos, s_lo, s_hi = tabs[0][...], tabs[1][...], tabs[2][...]
        return z * cos + pltpu.roll(z, LANES - half, 1) * s_lo + pltpu.roll(z, half, 1) * s_hi

    t64 = (c64, sl64, sh64)
    t32 = (c32, sl32, sh32)
    col = 0
    plan = ((qa_ref, (32, 32, 32, 32)), (nsa_ref, (32, None, 32, None)), (win_ref, (32, None)),
            (xb_ref, (None, None)), (yb_ref, (None, None)), (qc_ref, (16, 16)), (kc_ref, (16, 16)),
            (vc_ref, (None, None)), (ga_ref, (None,)))
    for out_ref, chunks in plan:
        z = seg(col, LANES * len(chunks))
        for c, half in enumerate(chunks):
            zc = z[:, c * LANES:(c + 1) * LANES]
            if half is not None:
                zc = rope(zc, half, t64 if half == 32 else t32)
            out_ref[:, c * LANES:(c + 1) * LANES] = zc
        col += LANES * len(chunks)


def _inproj(x2d, gain, w_r, tabs64, tabs32, seq_len, tm):
    n = x2d.shape[0]
    lt = tabs64[0].shape[0]
    if lt == 1:
        tab_spec = pl.BlockSpec((1, LANES), lambda i: (0, 0))
    else:
        per_seq = seq_len // tm
        tab_spec = pl.BlockSpec((tm, LANES), lambda i: (i % per_seq, 0))
    widths = (W_A, 4 * KVH_A * HD_A, 2 * KVH_A * HD_A, W_B, W_B, W_C, W_C, W_C, GATE_PAD)
    out_shape = tuple(jax.ShapeDtypeStruct((n, w), F32) for w in widths)
    out_specs = tuple(pl.BlockSpec((tm, w), lambda i: (i, 0)) for w in widths)
    return pl.pallas_call(
        _inproj_kernel,
        grid=(n // tm,),
        in_specs=[pl.BlockSpec((tm, D_MODEL), lambda i: (i, 0)),
                  pl.BlockSpec((1, D_MODEL), lambda i: (0, 0)),
                  pl.BlockSpec((D_MODEL, N_Z), lambda i: (0, 0))] + [tab_spec] * 6,
        out_specs=out_specs,
        out_shape=out_shape,
        compiler_params=_cparams(1),
        name="inproj",
    )(x2d, gain, w_r, *tabs64, *tabs32)


def _compress_rows(tok_refs, n_blocks, pe_ref, w1_ref, w2_ref):
    outs = []
    for kind, tok_ref in enumerate(tok_refs):
        xs = [(tok_ref[pl.ds(p, n_blocks, stride=CMP_BLOCK), :] + pe_ref[kind, p:p + 1, :]).astype(BF16)
              for p in range(CMP_BLOCK)]
        acc = jnp.dot(jnp.concatenate(xs, axis=1), w1_ref[kind], preferred_element_type=F32)
        hid = acc * jax.nn.sigmoid(acc)
        outs.append(jnp.dot(hid.astype(BF16), w2_ref[kind], preferred_element_type=F32))
    return outs


def _compress_kernel(k_ref, v_ref, pe_ref, w1_ref, w2_ref, out_ref):
    kvw = KVH_A * HD_A
    ck, cv = _compress_rows((k_ref, v_ref), out_ref.shape[0], pe_ref, w1_ref, w2_ref)
    out_ref[:, 0:kvw] = ck
    out_ref[:, kvw:2 * kvw] = cv


def _compress_weight_specs(const):
    kvw = KVH_A * HD_A
    return [pl.BlockSpec((2, CMP_BLOCK, kvw), const(3)),
            pl.BlockSpec((2, CMP_BLOCK * kvw, kvw), const(3)),
            pl.BlockSpec((2, kvw, kvw), const(3))]


def _compress(tok2d, pe2, w1bd, w2bd, t_tile):
    n_tok = tok2d.shape[0]
    kvw = KVH_A * HD_A
    return pl.pallas_call(
        _compress_kernel,
        grid=(n_tok // t_tile,),
        in_specs=[pl.BlockSpec((t_tile, kvw), lambda i: (i, 0)),
                  pl.BlockSpec((t_tile, kvw), lambda i: (i, 1))]
                 + _compress_weight_specs(lambda nd: (lambda i: (0,) * nd)),
        out_specs=pl.BlockSpec((t_tile // CMP_BLOCK, 2 * kvw), lambda i: (i, 0)),
        out_shape=jax.ShapeDtypeStruct((n_tok // CMP_BLOCK, 2 * kvw), F32),
        compiler_params=_cparams(1),
        name="compress",
    )(tok2d, tok2d, pe2, w1bd, w2bd)


def _compress_paged_kernel(layer, n_pages, chunk, pt_ref, cache_ref, pe_ref, w1_ref, w2_ref,
                           out_ref, raw, tok, sem):
    b = pl.program_id(0)
    n_chunks = n_pages // chunk
    page_rows = raw.shape[3]
    kvw = KVH_A * HD_A
    blocks_per_chunk = chunk * page_rows // CMP_BLOCK

    def page_copy(page, slot, j):
        return pltpu.make_async_copy(cache_ref.at[layer, page, pl.ds(0, 2 * kvw), :], raw.at[slot, j],
                                     sem.at[slot])

    def start(c, slot):
        for j in range(chunk):
            page_copy(pt_ref[b * n_pages + c * chunk + j], slot, j).start()

    def wait(slot):
        for j in range(chunk):
            page_copy(0, slot, j).wait()

    start(0, 0)
    for c in range(n_chunks):
        slot = c % 2
        if c + 1 < n_chunks:
            start(c + 1, 1 - slot)
        wait(slot)

        def to_token_major(j, carry):
            r0 = pl.multiple_of(j * page_rows, page_rows)
            for kind in range(2):
                tok[kind, pl.ds(r0, page_rows), :] = jnp.transpose(raw[slot, j, kind * kvw:(kind + 1) * kvw, :])
            return carry

        lax.fori_loop(0, chunk, to_token_major, 0)
        ck, cv = _compress_rows((tok.at[0], tok.at[1]), blocks_per_chunk, pe_ref, w1_ref, w2_ref)
        rows = slice(c * blocks_per_chunk, (c + 1) * blocks_per_chunk)
        out_ref[0, rows, 0:kvw] = ck
        out_ref[0, rows, kvw:2 * kvw] = cv


def _compress_paged(cache_t, layer, page_table_flat, n_seq, n_pages, pe2, w1bd, w2bd, chunk):
    page_rows = cache_t.shape[3]
    kvw = KVH_A * HD_A
    n_blocks = n_pages * page_rows // CMP_BLOCK
    grid_spec = pltpu.PrefetchScalarGridSpec(
        num_scalar_prefetch=1,
        grid=(n_seq,),
        in_specs=[pl.BlockSpec(memory_space=pl.ANY)]
                 + _compress_weight_specs(lambda nd: (lambda b, pt: (0,) * nd)),
        out_specs=pl.BlockSpec((1, n_blocks, 2 * kvw), lambda b, pt: (b, 0, 0)),
        scratch_shapes=[pltpu.VMEM((2, chunk, 2 * kvw, page_rows), F32),
                        pltpu.VMEM((2, chunk * page_rows, kvw), F32),
                        pltpu.SemaphoreType.DMA((2,))],
    )
    return pl.pallas_call(
        functools.partial(_compress_paged_kernel, layer, n_pages, chunk),
        grid_spec=grid_spec,
        out_shape=jax.ShapeDtypeStruct((n_seq, n_blocks, 2 * kvw), F32),
        compiler_params=_cparams(1),
        name="compress_paged",
    )(page_table_flat, cache_t, pe2, w1bd, w2bd)


def _cmp_branch(qg, ck_e, ck_o, cv_e, cv_o, m_e, m_o, n_rep):
    tq, n_s = m_e.shape
    single = tq == 1
    split = (lambda x: x) if single else (lambda x: x.reshape(n_rep, tq, n_s))
    merge = (lambda x: x) if single else (lambda x: x.reshape(n_rep * tq, n_s))
    if not single:
        m_e, m_o = m_e[None], m_o[None]
    s_e = jnp.where(m_e, split(_dot_nt(qg, ck_e)), NEG)
    s_o = jnp.where(m_o, split(_dot_nt(qg, ck_o)), NEG)
    mx = jnp.maximum(jnp.max(s_e, axis=-1, keepdims=True), jnp.max(s_o, axis=-1, keepdims=True))
    e_e = jnp.exp(s_e - mx)
    e_o = jnp.exp(s_o - mx)
    den = jnp.sum(e_e, axis=-1, keepdims=True) + jnp.sum(e_o, axis=-1, keepdims=True)
    p_e = jnp.where(m_e, e_e / den, 0.0)
    p_o = jnp.where(m_o, e_o / den, 0.0)
    o_c = _dot(merge(p_e), cv_e) + _dot(merge(p_o), cv_o)
    imp = jnp.sum(p_e, axis=0, keepdims=single) + jnp.sum(p_o, axis=0, keepdims=single)
    return o_c, imp


def _block_scores(imp, qpos, blk):
    cur = qpos // SEL_BLOCK
    valid = blk * SEL_BLOCK <= qpos
    forced = (blk == 0) | (blk == cur) | (blk == cur - 1)
    return jnp.where(valid, imp + jnp.where(forced, FORCE_BONUS, 0.0), -jnp.inf)


def _gate_mix_norm(o_c, o_s, o_w, gate, gn, tq):
    heads = []
    for g in range(KVH_A):
        for r in range(G_A):
            hh = g * G_A + r
            rows = slice(r * tq, (r + 1) * tq)
            o = (o_c[g][rows] * gate[:, 3 * hh:3 * hh + 1]
                 + o_s[g][rows] * gate[:, 3 * hh + 1:3 * hh + 2]
                 + o_w[g][rows] * gate[:, 3 * hh + 2:3 * hh + 3])
            heads.append(o)
    return _rms(jnp.concatenate(heads, axis=1), gn)


def _nsa_prompt_kernel(tq, tk, q_ref, ga_ref, cc_ref, sk_ref, sv_ref, wk_ref, wv_ref, gn_ref, out_ref):
    i = pl.program_id(1)
    q0 = i * tq
    n_s = cc_ref.shape[3]
    seq = sk_ref.shape[1]
    n_top = min(N_SEL, n_s)
    scale = HD_A ** -0.5
    q = q_ref[...] * scale
    q2 = q_ref[...] * (scale * LOG2E)
    qpos = q0 + _iota((tq, 1), 0)
    blk = _iota((tq, n_s), 1)
    m_e = blk * SEL_BLOCK + (CMP_BLOCK - 1) <= qpos
    m_o = blk * SEL_BLOCK + (2 * CMP_BLOCK - 1) <= qpos
    n_kt = (q0 + tq - 1) // tk + 1
    w_len = WINDOW + tq
    w_start = pl.multiple_of(jnp.maximum(q0 - WINDOW, 0), tq)
    wpos = w_start + _iota((tq, w_len), 1)
    rel = qpos - wpos
    m_w = (rel >= 0) & (rel < WINDOW)
    o_c, o_w, sels, qg2s = [], [], [], []
    for g in range(KVH_A):
        qg = jnp.concatenate([q[:, (g * G_A + r) * HD_A:(g * G_A + r + 1) * HD_A] for r in range(G_A)],
                             axis=0).astype(BF16)
        oc, imp = _cmp_branch(qg, cc_ref[0, 0, g], cc_ref[0, 1, g], cc_ref[0, 0, KVH_A + g],
                              cc_ref[0, 1, KVH_A + g], m_e, m_o, G_A)
        o_c.append(oc)
        score = _block_scores(imp, qpos, blk)
        rank = jnp.zeros((tq, n_s), jnp.int32)
        for j in range(n_s):
            col = score[:, j:j + 1]
            rank = rank + ((col > score) | ((col == score) & (blk > j))).astype(jnp.int32)
        sels.append((rank < n_top).astype(BF16))
        qg2s.append(jnp.concatenate([q2[:, (g * G_A + r) * HD_A:(g * G_A + r + 1) * HD_A] for r in range(G_A)],
                                    axis=0).astype(BF16))

        kw = wk_ref[0, pl.ds(w_start, w_len), g * HD_A:(g + 1) * HD_A]
        vw = wv_ref[0, pl.ds(w_start, w_len), g * HD_A:(g + 1) * HD_A]
        s = jnp.where(m_w[None], _dot_nt(qg, kw).reshape(G_A, tq, w_len), NEG)
        e = jnp.exp(s - jnp.max(s, axis=-1, keepdims=True))
        p = jnp.where(m_w[None], e / jnp.sum(e, axis=-1, keepdims=True), 0.0)
        o_w.append(_dot(p.reshape(G_A * tq, w_len), vw))

    hp = G_A // 2
    chains = [(g, c) for g in range(KVH_A) for c in range(G_A // hp)]

    def body(kt, carry):
        k0 = pl.multiple_of(kt * tk, tk)
        kpos = k0 + _iota((n_s, tk), 1)
        expand = (kpos // SEL_BLOCK == _iota((n_s, tk), 0)).astype(BF16)
        causal = k0 + _iota((tq, tk), 1) <= qpos
        new = []
        for g in range(KVH_A):
            k = sk_ref[0, pl.ds(k0, tk), g * HD_A:(g + 1) * HD_A].astype(BF16)
            v = sv_ref[0, pl.ds(k0, tk), g * HD_A:(g + 1) * HD_A].astype(BF16)
            keep = ((jnp.dot(sels[g], expand, preferred_element_type=F32) > 0.5) & causal)[None]
            for c in range(G_A // hp):
                m_run, l_run, acc = carry[len(new)]
                qh = qg2s[g][c * hp * tq:(c + 1) * hp * tq]
                s = jnp.where(keep, _dot_nt(qh, k).reshape(hp, tq, tk), NEG)
                m_new = jnp.maximum(m_run, jnp.max(s, axis=-1, keepdims=True))
                alpha = jnp.exp2(m_run - m_new)
                e = jnp.exp2(s - m_new)
                l_new = alpha * l_run + jnp.sum(e, axis=-1, keepdims=True)
                acc = alpha.reshape(hp * tq, 1) * acc + _dot(e.reshape(hp * tq, tk), v)
                new.append((m_new, l_new, acc))
        return tuple(new)

    init = tuple((jnp.full((hp, tq, 1), NEG, F32), jnp.zeros((hp, tq, 1), F32),
                  jnp.zeros((hp * tq, HD_A), F32)) for _ in chains)
    fin = lax.fori_loop(0, n_kt, body, init)
    o_s = []
    for g in range(KVH_A):
        parts = [fin[j][2] / fin[j][1].reshape(hp * tq, 1) for j, (gg, _) in enumerate(chains) if gg == g]
        o_s.append(jnp.concatenate(parts, axis=0))
    gate = jax.nn.sigmoid(ga_ref[...])
    out_ref[...] = _gate_mix_norm(o_c, o_s, o_w, gate, gn_ref[...], tq)


def _nsa_prompt(qa, ga, cc, nsa3, win3, gn, tq, tk):
    n = qa.shape[0]
    bsz, seq, _ = nsa3.shape
    per_seq = seq // tq
    n_s = cc.shape[3]
    kvw = KVH_A * HD_A
    return pl.pallas_call(
        functools.partial(_nsa_prompt_kernel, tq, tk),
        grid=(bsz, per_seq),
        in_specs=[pl.BlockSpec((tq, W_A), lambda b, i: (b * per_seq + i, 0)),
                  pl.BlockSpec((tq, GATE_PAD), lambda b, i: (b * per_seq + i, 0)),
                  pl.BlockSpec((1, 2, 2 * KVH_A, n_s, HD_A), lambda b, i: (b, 0, 0, 0, 0)),
                  pl.BlockSpec((1, seq, kvw), lambda b, i: (b, 0, 2)),
                  pl.BlockSpec((1, seq, kvw), lambda b, i: (b, 0, 3)),
                  pl.BlockSpec((1, seq, kvw), lambda b, i: (b, 0, 0)),
                  pl.BlockSpec((1, seq, kvw), lambda b, i: (b, 0, 1)),
                  pl.BlockSpec((1, W_A), lambda b, i: (0, 0))],
        out_specs=pl.BlockSpec((tq, W_A), lambda b, i: (b * per_seq + i, 0)),
        out_shape=jax.ShapeDtypeStruct((n, W_A), F32),
        compiler_params=_cparams(2),
        name="nsa_prompt",
    )(qa, ga, cc, nsa3, nsa3, win3, win3, gn)


def _lru_gates(xc, wa_ref, ba_ref, wx_ref, bx_ref, lam_ref):
    r = jax.nn.sigmoid(_dot(xc, wa_ref[...]) + ba_ref[...])
    ig = jax.nn.sigmoid(_dot(xc, wx_ref[...]) + bx_ref[...])
    log_a = -LRU_C * r * jax.nn.softplus(-lam_ref[...])
    a = jnp.exp(log_a)
    u = jnp.sqrt(jnp.tanh(-log_a) * (a * a + 1.0)) * (ig * xc)
    return a, u


def _lru_prompt_kernel(tl, xb_ref, yb_ref, cw_ref, cb_ref, wa_ref, ba_ref, wx_ref, bx_ref, lam_ref, gn_ref,
                       ob_ref, h_ref, cs_ref, xpad, h_carry):
    i = pl.program_id(1)

    @pl.when(i == 0)
    def _():
        xpad[0:SUBLANES, :] = jnp.zeros((SUBLANES, W_B), F32)
        h_carry[...] = jnp.zeros((1, W_B), F32)

    x = xb_ref[...]
    xpad[SUBLANES:SUBLANES + tl, :] = x
    xc = cb_ref[...]
    for k in range(LRU_CONV):
        off = SUBLANES - (LRU_CONV - 1) + k
        xc = xc + xpad[off:off + tl, :] * cw_ref[k:k + 1, :]
    a, u = _lru_gates(xc, wa_ref, ba_ref, wx_ref, bx_ref, lam_ref)
    row = _iota((tl, W_B), 0)
    s = 1
    while s < tl:
        a_sh = jnp.where(row >= s, pltpu.roll(a, s, 0), 1.0)
        u_sh = jnp.where(row >= s, pltpu.roll(u, s, 0), 0.0)
        u = a * u_sh + u
        a = a * a_sh
        s *= 2
    h = a * h_carry[...] + u
    h_carry[...] = h[tl - 1:tl, :]
    xpad[0:SUBLANES, :] = x[tl - SUBLANES:tl, :]
    ob_ref[...] = _rms(_gelu(yb_ref[...]) * h, gn_ref[...])
    h_ref[0] = h[tl - 1:tl, :]
    cs_ref[0] = x[tl - (LRU_CONV - 1):tl, :]


def _lru_prompt(xb, yb, bsz, seq, cw, cb, wa, ba, wx, bx, lam, gn, tl):
    n = xb.shape[0]
    per_seq = seq // tl
    row = lambda b, i: (b * per_seq + i, 0)
    const = lambda b, i: (0, 0)
    vec = pl.BlockSpec((1, W_B), const)
    return pl.pallas_call(
        functools.partial(_lru_prompt_kernel, tl),
        grid=(bsz, per_seq),
        in_specs=[pl.BlockSpec((tl, W_B), row), pl.BlockSpec((tl, W_B), row),
                  pl.BlockSpec((LRU_CONV, W_B), const), vec,
                  pl.BlockSpec((W_B, W_B), const), vec, pl.BlockSpec((W_B, W_B), const), vec, vec, vec],
        out_specs=(pl.BlockSpec((tl, W_B), row),
                   pl.BlockSpec((1, 1, W_B), lambda b, i: (b, 0, 0)),
                   pl.BlockSpec((1, LRU_CONV - 1, W_B), lambda b, i: (b, 0, 0))),
        out_shape=(jax.ShapeDtypeStruct((n, W_B), F32),
                   jax.ShapeDtypeStruct((bsz, 1, W_B), F32),
                   jax.ShapeDtypeStruct((bsz, LRU_CONV - 1, W_B), F32)),
        scratch_shapes=[pltpu.VMEM((SUBLANES + tl, W_B), F32), pltpu.VMEM((1, W_B), F32)],
        compiler_params=_cparams(2),
        name="lru_prompt",
    )(xb, yb, cw, cb, wa, ba, wx, bx, lam, gn)


def _lru_step_kernel(xb_ref, yb_ref, buf_ref, h0_ref, cw_ref, cb_ref, wa_ref, ba_ref, wx_ref, bx_ref,
                     lam_ref, gn_ref, ob_ref, h_ref):
    x = xb_ref[...]
    xc = cb_ref[...]
    for k in range(LRU_CONV - 1):
        xc = xc + buf_ref[k] * cw_ref[k:k + 1, :]
    xc = xc + x * cw_ref[LRU_CONV - 1:LRU_CONV, :]
    a, u = _lru_gates(xc, wa_ref, ba_ref, wx_ref, bx_ref, lam_ref)
    h = a * h0_ref[...] + u
    ob_ref[...] = _rms(_gelu(yb_ref[...]) * h, gn_ref[...])
    h_ref[...] = h


def _lru_step(xb, yb, buf_t, h0, cw, cb, wa, ba, wx, bx, lam, gn):
    n = xb.shape[0]
    return pl.pallas_call(
        _lru_step_kernel,
        out_shape=(jax.ShapeDtypeStruct((n, W_B), F32), jax.ShapeDtypeStruct((n, W_B), F32)),
        compiler_params=pltpu.CompilerParams(vmem_limit_bytes=VMEM_LIMIT),
        name="lru_step",
    )(xb, yb, buf_t, h0, cw, cb, wa, ba, wx, bx, lam, gn)


def _diff_lambda(dl_ref, lam_init):
    dl = dl_ref[...]
    s01 = jnp.sum(dl[0:1, :] * dl[1:2, :], axis=-1, keepdims=True)
    s23 = jnp.sum(dl[2:3, :] * dl[3:4, :], axis=-1, keepdims=True)
    return jnp.exp(s01) - jnp.exp(s23) + lam_init


def _diff_prompt_kernel(tq, tk, lam_init, q_ref, k_ref, v_ref, dl_ref, sub_ref, out_ref):
    i = pl.program_id(1)
    q0 = i * tq
    lam = _diff_lambda(dl_ref, lam_init)
    n_full = (q0 + 1) // tk
    n_kt = (q0 + tq - 1) // tk + 1
    qpos = q0 + _iota((tq, 1), 0)
    q = (q_ref[...] * (DC ** -0.5 * LOG2E)).astype(BF16)
    maps = [(h, c) for h in range(N_C) for c in range(2)]

    def make_body(masked):
        def body(kt, carry):
            k0 = pl.multiple_of(kt * tk, tk)
            kt_all = k_ref[0, pl.ds(k0, tk), :].astype(BF16)
            vt_all = v_ref[0, pl.ds(k0, tk), :].astype(BF16)
            keep = (k0 + _iota((tq, tk), 1) <= qpos) if masked else None
            new = []
            for (h, c), (m_run, l_run, acc) in zip(maps, carry):
                lo = h * DV_C + c * DC
                s = _dot_nt(q[:, lo:lo + DC], kt_all[:, lo:lo + DC])
                if masked:
                    s = jnp.where(keep, s, NEG)
                m_new = jnp.maximum(m_run, jnp.max(s, axis=-1, keepdims=True))
                alpha = jnp.exp2(m_run - m_new)
                e = jnp.exp2(s - m_new)
                new.append((m_new, alpha * l_run + jnp.sum(e, axis=-1, keepdims=True),
                            alpha * acc + _dot(e, vt_all[:, h * DV_C:(h + 1) * DV_C])))
            return tuple(new)
        return body

    init = tuple((jnp.full((tq, 1), NEG, F32), jnp.zeros((tq, 1), F32), jnp.zeros((tq, DV_C), F32))
                 for _ in maps)
    mid = lax.fori_loop(0, n_full, make_body(False), init)
    fin = lax.fori_loop(n_full, n_kt, make_body(True), mid)
    outs = []
    for h in range(N_C):
        res = [fin[2 * h + c][2] / fin[2 * h + c][1] for c in range(2)]
        od = res[0] - lam * res[1]
        outs.append(_rms(od, sub_ref[...]) * (1.0 - lam_init))
    out_ref[...] = jnp.concatenate(outs, axis=1)


def _diff_prompt(qc, kc3, vc3, dl, sub, lam_init, tq, tk):
    n = qc.shape[0]
    bsz, seq, _ = kc3.shape
    per_seq = seq // tq
    return pl.pallas_call(
        functools.partial(_diff_prompt_kernel, tq, tk, lam_init),
        grid=(bsz, per_seq),
        in_specs=[pl.BlockSpec((tq, W_C), lambda b, i: (b * per_seq + i, 0)),
                  pl.BlockSpec((1, seq, W_C), lambda b, i: (b, 0, 0)),
                  pl.BlockSpec((1, seq, W_C), lambda b, i: (b, 0, 0)),
                  pl.BlockSpec((4, DC), lambda b, i: (0, 0)),
                  pl.BlockSpec((1, DV_C), lambda b, i: (0, 0))],
        out_specs=pl.BlockSpec((tq, W_C), lambda b, i: (b * per_seq + i, 0)),
        out_shape=jax.ShapeDtypeStruct((n, W_C), F32),
        compiler_params=_cparams(2),
        name="diff_prompt",
    )(qc, kc3, vc3, dl, sub)


def _diff_paged_kernel(kp, lam_init, pt_ref, q_ref, kn_ref, vn_ref, dl_ref, sub_ref, *rest):
    k_refs, v_refs = rest[:kp], rest[kp:2 * kp]
    out_ref, m_sc, l_sc, acc_sc = rest[2 * kp:]
    c = pl.program_id(1)
    n_rows = 2 * N_C

    @pl.when(c == 0)
    def _():
        m_sc[...] = jnp.full((n_rows, 1), NEG, F32)
        l_sc[...] = jnp.zeros((n_rows, 1), F32)
        acc_sc[...] = jnp.zeros((n_rows, W_C), F32)

    own = _iota((n_rows, W_C), 1) // DC == _iota((n_rows, W_C), 0)
    q_bd = jnp.where(own, jnp.broadcast_to(q_ref[0], (n_rows, W_C)), 0.0) * (DC ** -0.5)
    q_bf = q_bd.astype(BF16)
    s = jnp.concatenate([_dot(q_bf, k_refs[j][0]) for j in range(kp)], axis=1)
    m_run = m_sc[...]
    m_new = jnp.maximum(m_run, jnp.max(s, axis=-1, keepdims=True))
    alpha = jnp.exp(m_run - m_new)
    e = jnp.exp(s - m_new)
    page = k_refs[0].shape[2]
    acc = alpha * acc_sc[...]
    for j in range(kp):
        acc = acc + _dot_nt(e[:, j * page:(j + 1) * page], v_refs[j][0])
    l_new = alpha * l_sc[...] + jnp.sum(e, axis=-1, keepdims=True)
    m_sc[...] = m_new
    l_sc[...] = l_new
    acc_sc[...] = acc

    @pl.when(c == pl.num_programs(1) - 1)
    def _():
        s_n = jnp.sum(q_bd * kn_ref[0], axis=-1, keepdims=True)
        m_f = jnp.maximum(m_new, s_n)
        al = jnp.exp(m_new - m_f)
        e_n = jnp.exp(s_n - m_f)
        o = (al * acc + e_n * vn_ref[0]) / (al * l_new + e_n)
        lam = _diff_lambda(dl_ref, lam_init)
        lane_head = _iota((n_rows, W_C), 1) // DV_C
        rowi = _iota((n_rows, W_C), 0)
        comb = jnp.where(rowi == 2 * lane_head, o, 0.0) - lam * jnp.where(rowi == 2 * lane_head + 1, o, 0.0)
        od = jnp.sum(comb, axis=0, keepdims=True)
        heads = [_rms(od[:, h * DV_C:(h + 1) * DV_C], sub_ref[...]) * (1.0 - lam_init) for h in range(N_C)]
        out_ref[0] = jnp.concatenate(heads, axis=1)


def _diff_paged(qc, kc_new, vc_new, cache_k4, cache_v4, layer, page_table_flat, n_pages, dl, sub, lam_init, kp):
    n_seq = qc.shape[0]
    page = cache_k4.shape[3]

    def page_spec(j):
        return pl.BlockSpec((None, 1, W_C, page),
                            lambda b, c, pt: (layer, pt[b * n_pages + c * kp + j], 0, 0))

    row = pl.BlockSpec((1, 1, W_C), lambda b, c, pt: (b, 0, 0))
    grid_spec = pltpu.PrefetchScalarGridSpec(
        num_scalar_prefetch=1,
        grid=(n_seq, n_pages // kp),
        in_specs=[row, row, row,
                  pl.BlockSpec((4, DC), lambda b, c, pt: (0, 0)),
                  pl.BlockSpec((1, DV_C), lambda b, c, pt: (0, 0))]
                 + [page_spec(j) for j in range(kp)] + [page_spec(j) for j in range(kp)],
        out_specs=row,
        scratch_shapes=[pltpu.VMEM((2 * N_C, 1), F32), pltpu.VMEM((2 * N_C, 1), F32),
                        pltpu.VMEM((2 * N_C, W_C), F32)],
    )
    return pl.pallas_call(
        functools.partial(_diff_paged_kernel, kp, lam_init),
        grid_spec=grid_spec,
        out_shape=jax.ShapeDtypeStruct((n_seq, 1, W_C), F32),
        compiler_params=_cparams(2),
        name="diff_paged",
    )(page_table_flat, qc, kc_new, vc_new, dl, sub, *([cache_k4] * kp), *([cache_v4] * kp))


def _nsa_step_a_kernel(qpos, n_s, q_ref, cc_ref, wb_ref, wn_ref, oc_ref, ow_ref, idx_ref):
    nsp = cc_ref.shape[3]
    n_buf = wb_ref.shape[4]
    q = q_ref[0] * (HD_A ** -0.5)
    blk = _iota((1, nsp), 1)
    real = blk < n_s
    m_e = real & (blk * SEL_BLOCK + (CMP_BLOCK - 1) <= qpos)
    m_o = real & (blk * SEL_BLOCK + (2 * CMP_BLOCK - 1) <= qpos)
    kpos = qpos - n_buf + _iota((1, n_buf), 1)
    rel = qpos - kpos
    m_w = (rel >= 0) & (rel < WINDOW) & (kpos >= 0)
    rows = _iota((nsp, nsp), 0)
    cols = _iota((nsp, nsp), 1)
    lane = _iota((1, LANES), 1)
    oc_parts, ow_parts, idx_rows = [], [], []
    for g in range(KVH_A):
        qg = jnp.concatenate([q[:, (g * G_A + r) * HD_A:(g * G_A + r + 1) * HD_A] for r in range(G_A)], axis=0)
        oc, imp = _cmp_branch(qg, cc_ref[0, 0, g], cc_ref[0, 1, g], cc_ref[0, 0, KVH_A + g],
                              cc_ref[0, 1, KVH_A + g], m_e, m_o, G_A)
        oc_parts.append(oc)
        score = jnp.where(real, _block_scores(imp, qpos, blk), -jnp.inf)
        s_row = jnp.broadcast_to(score, (nsp, nsp))
        s_col = jnp.transpose(s_row)
        beats = (s_col > s_row) | ((s_col == s_row) & (rows < cols))
        rank = jnp.sum(beats.astype(jnp.int32), axis=0, keepdims=True)
        idx = jnp.zeros((1, LANES), jnp.int32)
        blk_f = blk.astype(F32)
        for r in range(min(N_SEL, n_s)):
            pick = jnp.sum(jnp.where(rank == r, blk_f, 0.0), axis=-1, keepdims=True).astype(jnp.int32)
            idx = jnp.where(lane == r, pick, idx)
        idx_rows.append(idx)
        kb_t = wb_ref[0, 0, g]
        vb_t = wb_ref[0, 1, g]
        kn = wn_ref[0, :, g * HD_A:(g + 1) * HD_A]
        vn = wn_ref[0, :, (KVH_A + g) * HD_A:(KVH_A + g + 1) * HD_A]
        s_b = jnp.where(m_w, _dot(qg, kb_t), NEG)
        s_n = jnp.sum(qg.astype(BF16).astype(F32) * kn.astype(BF16).astype(F32), axis=-1, keepdims=True)
        mx = jnp.maximum(jnp.max(s_b, axis=-1, keepdims=True), s_n)
        e_b = jnp.exp(s_b - mx)
        e_n = jnp.exp(s_n - mx)
        den = jnp.sum(e_b, axis=-1, keepdims=True) + e_n
        p_b = jnp.where(m_w, e_b / den, 0.0)
        ow_parts.append(_dot_nt(p_b, vb_t) + (e_n / den) * vn)
    oc_ref[0] = jnp.concatenate(oc_parts, axis=0)
    ow_ref[0] = jnp.concatenate(ow_parts, axis=0)
    idx_ref[0] = jnp.concatenate(idx_rows + [jnp.zeros((SUBLANES - KVH_A, LANES), jnp.int32)], axis=0)


def _nsa_step_a(qa3, cc, win_cache_t, layer, win_new3, qpos, n_s):
    n_seq = qa3.shape[0]
    nsp = cc.shape[3]
    n_buf = win_cache_t.shape[5]
    kvw = 2 * KVH_A * HD_A
    return pl.pallas_call(
        functools.partial(_nsa_step_a_kernel, qpos, n_s),
        grid=(n_seq,),
        in_specs=[pl.BlockSpec((1, 1, W_A), lambda b: (b, 0, 0)),
                  pl.BlockSpec((1, 2, 2 * KVH_A, nsp, HD_A), lambda b: (b, 0, 0, 0, 0)),
                  pl.BlockSpec((None, 1, 2, KVH_A, HD_A, n_buf), lambda b: (layer, b, 0, 0, 0, 0)),
                  pl.BlockSpec((1, 1, kvw), lambda b: (b, 0, 0))],
        out_specs=(pl.BlockSpec((1, H_A, HD_A), lambda b: (b, 0, 0)),
                   pl.BlockSpec((1, H_A, HD_A), lambda b: (b, 0, 0)),
                   pl.BlockSpec((1, SUBLANES, LANES), lambda b: (b, 0, 0))),
        out_shape=(jax.ShapeDtypeStruct((n_seq, H_A, HD_A), F32),
                   jax.ShapeDtypeStruct((n_seq, H_A, HD_A), F32),
                   jax.ShapeDtypeStruct((n_seq, SUBLANES, LANES), jnp.int32)),
        compiler_params=_cparams(1),
        name="nsa_step_rank",
    )(qa3, cc, win_cache_t, win_new3)


def _nsa_step_b_kernel(n_top, n_past, blocks_per_page, idx_ref, pt_ref, q_ref, oc_ref, ow_ref, ga_ref, new_ref, gn_ref, *rest):
    nk = KVH_A * n_top
    k_refs, v_refs = rest[:nk], rest[nk:2 * nk]
    out_ref = rest[2 * nk]
    b = pl.program_id(0)
    q = q_ref[0] * (HD_A ** -0.5)
    o_s = []
    for g in range(KVH_A):
        qg = jnp.concatenate([q[:, (g * G_A + r) * HD_A:(g * G_A + r + 1) * HD_A] for r in range(G_A)], axis=0)
        page = k_refs[0].shape[4]
        lane_blk = _iota((1, page), 1) // SEL_BLOCK
        s_parts = []
        has_tail = False
        for j in range(n_top):
            blk = idx_ref[(b * KVH_A + g) * n_top + j]
            s_j = _dot(qg, k_refs[g * n_top + j][0, 0, 0])
            keep = (lane_blk == blk % blocks_per_page) & (blk < n_past)
            s_parts.append(jnp.where(keep, s_j, NEG))
            has_tail = (blk == n_past) | has_tail
        s = jnp.concatenate(s_parts, axis=1)
        kn = new_ref[0, :, 2 * KVH_A * HD_A + g * HD_A:2 * KVH_A * HD_A + (g + 1) * HD_A]
        vn = new_ref[0, :, 3 * KVH_A * HD_A + g * HD_A:3 * KVH_A * HD_A + (g + 1) * HD_A]
        s_n = jnp.sum(qg.astype(BF16).astype(F32) * kn.astype(BF16).astype(F32), axis=-1, keepdims=True)
        s_n = jnp.where(has_tail, s_n, NEG)
        mx = jnp.maximum(jnp.max(s, axis=-1, keepdims=True), s_n)
        e = jnp.exp(s - mx)
        e_n = jnp.exp(s_n - mx)
        den = jnp.sum(e, axis=-1, keepdims=True) + e_n
        e = jnp.where(s > 0.5 * NEG, e, 0.0)
        e_n = jnp.where(s_n > 0.5 * NEG, e_n, 0.0)
        acc = (e_n / den) * vn
        for j in range(n_top):
            acc = acc + _dot_nt(e[:, j * page:(j + 1) * page] / den, v_refs[g * n_top + j][0, 0, 0])
        o_s.append(acc)
    oc = oc_ref[0]
    ow = ow_ref[0]
    gate = jax.nn.sigmoid(ga_ref[0])
    out_ref[0] = _gate_mix_norm([oc[0:G_A], oc[G_A:2 * G_A]], o_s, [ow[0:G_A], ow[G_A:2 * G_A]],
                                gate, gn_ref[...], 1)


def _nsa_step_select(idx_flat, page_table_flat, n_pages, cache_t6, layer,
                     qa3, oc, ow, ga3, nsa_new3, gn, n_top):
    n_seq = qa3.shape[0]
    kvw = KVH_A * HD_A
    page_rows = cache_t6.shape[5]
    blocks_per_page = page_rows // SEL_BLOCK
    n_past = n_pages * blocks_per_page

    def blk_spec(g, j, kind):
        def imap(b, idx, pt):
            blk = jnp.minimum(idx[(b * KVH_A + g) * n_top + j], n_past - 1)
            return (layer, pt[b * n_pages + blk // blocks_per_page], kind, g, 0, 0)
        return pl.BlockSpec((None, 1, 1, 1, HD_A, page_rows), imap)

    def row(w):
        return pl.BlockSpec((1, 1, w), lambda b, idx, pt: (b, 0, 0))

    head = pl.BlockSpec((1, H_A, HD_A), lambda b, idx, pt: (b, 0, 0))
    pairs = [(g, j) for g in range(KVH_A) for j in range(n_top)]
    grid_spec = pltpu.PrefetchScalarGridSpec(
        num_scalar_prefetch=2,
        grid=(n_seq,),
        in_specs=[row(W_A), head, head, row(GATE_PAD), row(4 * kvw),
                  pl.BlockSpec((1, W_A), lambda b, idx, pt: (0, 0))]
                 + [blk_spec(g, j, 2) for g, j in pairs] + [blk_spec(g, j, 3) for g, j in pairs],
        out_specs=row(W_A),
    )
    n_in = 2 * len(pairs)
    return pl.pallas_call(
        functools.partial(_nsa_step_b_kernel, n_top, n_past, blocks_per_page),
        grid_spec=grid_spec,
        out_shape=jax.ShapeDtypeStruct((n_seq, 1, W_A), F32),
        compiler_params=_cparams(1),
        name="nsa_step_select",
    )(idx_flat, page_table_flat, qa3, oc, ow, ga3, nsa_new3, gn, *([cache_t6] * n_in))


def _ffn_chunks():
    fc = 256
    assert D_FF % fc == 0
    return fc, D_FF // fc


def _ffn_tail(x1, nf_ref, wup_ref, wdn_ref, conv_fn, final_gain, act_sc):
    fc, n_fc = _ffn_chunks()
    h2 = _rms(x1, nf_ref[...]).astype(BF16)
    for c in range(n_fc):
        g = jnp.dot(h2, wup_ref[:, c * fc:(c + 1) * fc], preferred_element_type=F32)
        v = jnp.dot(h2, wup_ref[:, D_FF + c * fc:D_FF + (c + 1) * fc], preferred_element_type=F32)
        act_sc[:, c * fc:(c + 1) * fc] = (_gelu(conv_fn(c, g)) * v).astype(BF16)
    y = x1 + jnp.dot(act_sc[...], wdn_ref[...], preferred_element_type=F32)
    if final_gain is not None:
        y = _rms(y, final_gain[...])
    return y


def _mix_residual(x_ref, oa_ref, ob_ref, oc_ref, wo_ref, mix_sc):
    mix_sc[:, 0:W_A] = oa_ref[...].astype(BF16)
    mix_sc[:, W_A:W_A + W_B] = ob_ref[...].astype(BF16)
    mix_sc[:, W_A + W_B:] = oc_ref[...].astype(BF16)
    return x_ref[...] + jnp.dot(mix_sc[...], wo_ref[...], preferred_element_type=F32)


def _ffn_prompt_kernel(tm, last, x_ref, oa_ref, ob_ref, oc_ref, wo_ref, nf_ref, wup_ref, cw_ref, cb_ref,
                       wdn_ref, fin_ref, y_ref, st_ref, gpad, carry, mix_sc, act_sc):
    i = pl.program_id(1)
    fc, _ = _ffn_chunks()

    @pl.when(i == 0)
    def _():
        carry[...] = jnp.zeros(carry.shape, F32)

    def conv(c, g):
        cols = slice(c * fc, (c + 1) * fc)
        gpad[0:SUBLANES, :] = carry[:, cols]
        gpad[SUBLANES:SUBLANES + tm, :] = g
        out = cb_ref[:, cols]
        for k in range(FFN_CONV):
            off = SUBLANES - (FFN_CONV - 1) + k
            out = out + gpad[off:off + tm, :] * cw_ref[k:k + 1, cols]
        carry[:, cols] = g[tm - SUBLANES:tm, :]
        return out

    x1 = _mix_residual(x_ref, oa_ref, ob_ref, oc_ref, wo_ref, mix_sc)
    y_ref[...] = _ffn_tail(x1, nf_ref, wup_ref, wdn_ref, conv, fin_ref if last else None, act_sc)
    st_ref[0] = carry[SUBLANES - (FFN_CONV - 1):SUBLANES, :]


def _ffn_prompt(x2d, oa, ob, oc, wo, nf, wup, cw, cb, wdn, fin, bsz, seq, tm, last):
    n = x2d.shape[0]
    per_seq = seq // tm
    fc, _ = _ffn_chunks()
    row = lambda b, i: (b * per_seq + i, 0)
    const = lambda b, i: (0, 0)
    one = pl.Buffered(1)
    return pl.pallas_call(
        functools.partial(_ffn_prompt_kernel, tm, last),
        grid=(bsz, per_seq),
        in_specs=[pl.BlockSpec((tm, D_MODEL), row), pl.BlockSpec((tm, W_A), row),
                  pl.BlockSpec((tm, W_B), row), pl.BlockSpec((tm, W_C), row),
                  pl.BlockSpec((D_MODEL, D_MODEL), const, pipeline_mode=one),
                  pl.BlockSpec((1, D_MODEL), const),
                  pl.BlockSpec((D_MODEL, 2 * D_FF), const, pipeline_mode=one),
                  pl.BlockSpec((FFN_CONV, D_FF), const), pl.BlockSpec((1, D_FF), const),
                  pl.BlockSpec((D_FF, D_MODEL), const, pipeline_mode=one),
                  pl.BlockSpec((1, D_MODEL), const)],
        out_specs=(pl.BlockSpec((tm, D_MODEL), row),
                   pl.BlockSpec((1, FFN_CONV - 1, D_FF), lambda b, i: (b, 0, 0))),
        out_shape=(jax.ShapeDtypeStruct((n, D_MODEL), F32),
                   jax.ShapeDtypeStruct((bsz, FFN_CONV - 1, D_FF), F32)),
        scratch_shapes=[pltpu.VMEM((SUBLANES + tm, fc), F32), pltpu.VMEM((SUBLANES, D_FF), F32),
                        pltpu.VMEM((tm, D_MODEL), BF16), pltpu.VMEM((tm, D_FF), BF16)],
        compiler_params=_cparams(2),
        name="ffn_prompt",
    )(x2d, oa, ob, oc, wo, nf, wup, cw, cb, wdn, fin)


def _ffn_step_kernel(last, x_ref, oa_ref, ob_ref, oc_ref, wo_ref, nf_ref, wup_ref, cw_ref, cb_ref, wdn_ref,
                     fin_ref, buf_ref, y_ref, g_ref, mix_sc, act_sc):
    fc, _ = _ffn_chunks()

    def conv(c, g):
        cols = slice(c * fc, (c + 1) * fc)
        g_ref[:, cols] = g
        out = cb_ref[:, cols]
        for k in range(FFN_CONV - 1):
            out = out + buf_ref[k, :, cols] * cw_ref[k:k + 1, cols]
        return out + g * cw_ref[FFN_CONV - 1:FFN_CONV, cols]

    x1 = _mix_residual(x_ref, oa_ref, ob_ref, oc_ref, wo_ref, mix_sc)
    y_ref[...] = _ffn_tail(x1, nf_ref, wup_ref, wdn_ref, conv, fin_ref if last else None, act_sc)


def _ffn_step(x2d, oa, ob, oc, wo, nf, wup, cw, cb, wdn, fin, buf_t, last):
    n = x2d.shape[0]
    return pl.pallas_call(
        functools.partial(_ffn_step_kernel, last),
        out_shape=(jax.ShapeDtypeStruct((n, D_MODEL), F32), jax.ShapeDtypeStruct((n, D_FF), F32)),
        scratch_shapes=[pltpu.VMEM((n, D_MODEL), BF16), pltpu.VMEM((n, D_FF), BF16)],
        compiler_params=pltpu.CompilerParams(vmem_limit_bytes=VMEM_LIMIT),
        name="ffn_step",
    )(x2d, oa, ob, oc, wo, nf, wup, cw, cb, wdn, fin, buf_t)


def _block_diag(blocks):
    nb, k, n = blocks.shape[-3:]
    eye = jnp.eye(nb, dtype=blocks.dtype)
    out = blocks[..., :, :, None, :] * eye[:, None, :, None]
    return out.reshape(blocks.shape[:-3] + (nb * k, nb * n))


def _layer_weights(l, w_in, cmp_pe, cmp_w1, cmp_w2, lru_gate_a_w, lru_gate_x_w, w_out, w_up, w_down):
    w = w_in[l]
    g0 = W_A + 3 * 2 * KVH_A * HD_A
    w_r = jnp.concatenate([w[:, :g0], w[:, g0 + N_GATE:], w[:, g0:g0 + N_GATE],
                           jnp.zeros((D_MODEL, GATE_PAD - N_GATE), F32)], axis=1).astype(BF16)
    pe = cmp_pe[l]
    pe4 = jnp.concatenate([pe] * KVH_A, axis=-1)
    w1 = cmp_w1[l]
    w1bd = _block_diag(jnp.stack([w1] * KVH_A, axis=2)).astype(BF16)
    w1bd = w1bd.reshape(2, CMP_BLOCK * KVH_A * HD_A, KVH_A * HD_A)
    w2 = cmp_w2[l]
    w2bd = _block_diag(jnp.stack([w2] * KVH_A, axis=1)).astype(BF16)
    wa = _block_diag(lru_gate_a_w[l]).astype(BF16)
    wx = _block_diag(lru_gate_x_w[l]).astype(BF16)
    return dict(w_r=w_r, pe4=pe4, w1bd=w1bd, w2bd=w2bd, wa=wa, wx=wx,
                wo=w_out[l].astype(BF16), wup=w_up[l].astype(BF16), wdn=w_down[l].astype(BF16))


def _split_compressed(c, n_s_pad):
    bsz, n_c, _ = c.shape
    n_s = n_c // 2
    cc = c.reshape(bsz, n_s, 2, 2 * KVH_A, HD_A).transpose(0, 2, 3, 1, 4)
    if n_s_pad > n_s:
        cc = jnp.pad(cc, ((0, 0), (0, 0), (0, 0), (0, n_s_pad - n_s), (0, 0)))
    return cc


def kernel(x_prompt, x_sample, cache_nsa_kv, cache_diff_k, cache_diff_v, cache_win_kv, state_lru_h,
           state_lru_conv, state_ffn_conv, page_table, norm_mix, w_in, cmp_pe, cmp_w1, cmp_w2, gn_nsa,
           lru_conv_w, lru_conv_b, lru_gate_a_w, lru_gate_a_b, lru_gate_x_w, lru_gate_x_b, lru_lambda,
           gn_lru, diff_lambda, diff_subln, w_out, norm_ffn, w_up, ffn_conv_w, ffn_conv_b, w_down, norm_final):
    bsz, seq, _ = x_prompt.shape
    n_seq, dec_len, _ = x_sample.shape
    depth, n_pool, page, _, _, _ = cache_nsa_kv.shape
    n_pages = page_table.shape[1]
    past_len = n_pages * page
    assert dec_len == 1 and seq % SEL_BLOCK == 0 and past_len % SEL_BLOCK == 0
    assert seq >= WINDOW + 128 and cache_win_kv.shape[2] == min(WINDOW, past_len)
    n_tok = bsz * seq
    kvw = KVH_A * HD_A

    cache_t6 = jnp.transpose(cache_nsa_kv, (0, 1, 3, 4, 5, 2))
    cache_tok = cache_t6.reshape(depth, n_pool, 4 * kvw, page)
    cache_dk = jnp.transpose(cache_diff_k, (0, 1, 3, 4, 5, 2)).reshape(depth, n_pool, W_C, page)
    cache_dv = jnp.transpose(cache_diff_v, (0, 1, 3, 4, 2)).reshape(depth, n_pool, W_C, page)
    cache_win = jnp.transpose(cache_win_kv, (0, 1, 3, 4, 5, 2))
    pt_flat = page_table.reshape(-1)

    tabs_p64 = _rope_tables(jnp.arange(seq), HD_A)
    tabs_p32 = _rope_tables(jnp.arange(seq), DC)
    pos_s = past_len + jnp.arange(1)
    tabs_s64 = _rope_tables(pos_s, HD_A)
    tabs_s32 = _rope_tables(pos_s, DC)

    n_s_dec = (past_len + SEL_BLOCK) // SEL_BLOCK
    n_s_pad = -(-n_s_dec // LANES) * LANES
    n_top_dec = min(N_SEL, n_s_dec)

    xp = x_prompt.reshape(n_tok, D_MODEL)
    xs = x_sample.reshape(n_seq, D_MODEL)
    fin = norm_final.reshape(1, D_MODEL)
    st_p = [[] for _ in range(7)]
    st_s = [[] for _ in range(7)]
    for l in range(depth):
        last = l == depth - 1
        lam_init = 0.8 - 0.6 * math.exp(-0.3 * l)
        lw = _layer_weights(l, w_in, cmp_pe, cmp_w1, cmp_w2, lru_gate_a_w, lru_gate_x_w, w_out, w_up, w_down)
        gain = norm_mix[l].reshape(1, D_MODEL)
        gn = gn_nsa[l].reshape(1, W_A)
        lru_args = (lru_conv_w[l], lru_conv_b[l].reshape(1, W_B), lw["wa"], lru_gate_a_b[l].reshape(1, W_B),
                    lw["wx"], lru_gate_x_b[l].reshape(1, W_B), lru_lambda[l].reshape(1, W_B),
                    gn_lru[l].reshape(1, W_B))
        dl = diff_lambda[l]
        sub = diff_subln[l].reshape(1, DV_C)
        ffn_args = (lw["wo"], norm_ffn[l].reshape(1, D_MODEL), lw["wup"], ffn_conv_w[l],
                    ffn_conv_b[l].reshape(1, D_FF), lw["wdn"], fin)

        qa, nsa, win, xb, yb, qc, kc, vc, ga = _inproj(xp, gain, lw["w_r"], tabs_p64, tabs_p32, seq, 512)
        comp = _compress(nsa, lw["pe4"], lw["w1bd"], lw["w2bd"], math.gcd(n_tok, 4096))
        cc = _split_compressed(comp.reshape(bsz, seq // CMP_BLOCK, 4 * HD_A), seq // SEL_BLOCK)
        nsa3 = nsa.reshape(bsz, seq, 4 * kvw)
        win3 = win.reshape(bsz, seq, 2 * kvw)
        oa = _nsa_prompt(qa, ga, cc, nsa3, win3, gn, 128, 256)
        ob, h_last, conv_st = _lru_prompt(xb, yb, bsz, seq, *lru_args, 256)
        kc3 = kc.reshape(bsz, seq, W_C)
        vc3 = vc.reshape(bsz, seq, W_C)
        oc = _diff_prompt(qc, kc3, vc3, dl, sub, lam_init, 256, 512)
        xp, ffn_st = _ffn_prompt(xp, oa, ob, oc, *ffn_args, bsz, seq, 256, last)
        n_win = min(WINDOW, seq)
        st_p[0].append(nsa.reshape(bsz, seq, 4, KVH_A, HD_A))
        st_p[1].append(win3[:, seq - n_win:].reshape(bsz, n_win, 2, KVH_A, HD_A))
        st_p[2].append(kc.reshape(bsz, seq, N_C, 2, DC))
        st_p[3].append(vc.reshape(bsz, seq, N_C, DV_C))
        st_p[4].append(h_last.reshape(bsz, W_B))
        st_p[5].append(conv_st)
        st_p[6].append(ffn_st)

        qa, nsa, win, xb, yb, qc, kc, vc, ga = _inproj(xs, gain, lw["w_r"], tabs_s64, tabs_s32, 1, n_seq)
        comp_past = _compress_paged(cache_tok, l, pt_flat, n_seq, n_pages, lw["pe4"], lw["w1bd"], lw["w2bd"],
                                    math.gcd(n_pages, 32))
        tail = jnp.pad(nsa.reshape(n_seq, 1, 4 * kvw), ((0, 0), (0, SEL_BLOCK - 1), (0, 0)))
        comp_tail = _compress(tail.reshape(n_seq * SEL_BLOCK, 4 * kvw), lw["pe4"], lw["w1bd"], lw["w2bd"],
                              n_seq * SEL_BLOCK)
        comp = jnp.concatenate([comp_past, comp_tail.reshape(n_seq, SEL_BLOCK // CMP_BLOCK, 4 * HD_A)], axis=1)
        cc = _split_compressed(comp, n_s_pad)
        qa3 = qa.reshape(n_seq, 1, W_A)
        o_cmp, o_win, idx = _nsa_step_a(qa3, cc, cache_win, l, win.reshape(n_seq, 1, 2 * kvw), past_len, n_s_dec)
        idx_flat = idx[:, :KVH_A, :n_top_dec].reshape(-1)
        oa = _nsa_step_select(idx_flat, pt_flat, n_pages, cache_t6, l, qa3, o_cmp, o_win,
                              ga.reshape(n_seq, 1, GATE_PAD), nsa.reshape(n_seq, 1, 4 * kvw), gn,
                              n_top_dec).reshape(n_seq, W_A)
        buf_t = jnp.transpose(state_lru_conv[l], (1, 0, 2))
        ob, h_new = _lru_step(xb, yb, buf_t, state_lru_h[l], *lru_args)
        oc = _diff_paged(qc.reshape(n_seq, 1, W_C), kc.reshape(n_seq, 1, W_C), vc.reshape(n_seq, 1, W_C),
                         cache_dk, cache_dv, l, pt_flat, n_pages, dl, sub, lam_init,
                         math.gcd(n_pages, 16)).reshape(n_seq, W_C)
        fbuf_t = jnp.transpose(state_ffn_conv[l], (1, 0, 2))
        xs, g_new = _ffn_step(xs, oa, ob, oc, *ffn_args, fbuf_t, last)
        st_s[0].append(nsa.reshape(n_seq, 1, 4, KVH_A, HD_A))
        st_s[1].append(win.reshape(n_seq, 1, 2, KVH_A, HD_A))
        st_s[2].append(kc.reshape(n_seq, 1, N_C, 2, DC))
        st_s[3].append(vc.reshape(n_seq, 1, N_C, DV_C))
        st_s[4].append(h_new)
        st_s[5].append(jnp.concatenate([state_lru_conv[l][:, 1:], xb[:, None, :]], axis=1))
        st_s[6].append(jnp.concatenate([state_ffn_conv[l][:, 1:], g_new[:, None, :]], axis=1))

    y_prompt = xp.reshape(bsz, seq, D_MODEL)
    y_sample = xs.reshape(n_seq, 1, D_MODEL)
    return (y_prompt, y_sample) + tuple(jnp.stack(s, axis=0) for s in st_p) + tuple(
        jnp.stack(s, axis=0) for s in st_s)
```

```python
import functools
import math

import jax
import jax.numpy as jnp
from jax import lax
from jax.experimental import pallas as pl
from jax.experimental.pallas import tpu as pltpu

F32 = jnp.float32
BF16 = jnp.bfloat16

D_MODEL = 1024
HD_A = 64
W_A = D_MODEL // 2
H_A = W_A // HD_A
KVH_A = 2
G_A = H_A // KVH_A
CMP_BLOCK = 32
SEL_BLOCK = 64
N_SEL = 16
WINDOW = 512
FORCE_BONUS = 1.0e4
W_B = D_MODEL // 4
LRU_BLOCKS = 4
LRU_CONV = 4
LRU_C = 8.0
W_C = D_MODEL // 4
N_C = 4
DV_C = W_C // N_C
DC = DV_C // 2
D_FF = ((8 * D_MODEL // 3 + 127) // 128) * 128
FFN_CONV = 3
ROPE_THETA = 10000.0
EPS = 1e-6
NEG = -1e30
LOG2E = 1.4426950408889634

LANES = 128
SUBLANES = 8
N_GATE = 3 * H_A
GATE_PAD = LANES
N_Z = W_A + 3 * 2 * KVH_A * HD_A + 5 * W_B + GATE_PAD
VMEM_LIMIT = 56 * 1024 * 1024


def _cparams(n_axes):
    return pltpu.CompilerParams(dimension_semantics=("arbitrary",) * n_axes,
                                vmem_limit_bytes=VMEM_LIMIT)


def _rms(x, g):
    return x * lax.rsqrt(jnp.mean(x * x, axis=-1, keepdims=True) + EPS) * g


def _dot(a, b):
    return jnp.dot(a.astype(BF16), b.astype(BF16), preferred_element_type=F32)


def _dot_nt(a, b):
    return lax.dot_general(a.astype(BF16), b.astype(BF16), (((1,), (1,)), ((), ())),
                           preferred_element_type=F32)


def _iota(shape, dim):
    return lax.broadcasted_iota(jnp.int32, shape, dim)


def _gelu(x):
    return jax.nn.gelu(x)


def _rope_tables(pos, d):
    inv = ROPE_THETA ** (-jnp.arange(0, d, 2, dtype=F32) / d)
    ang = pos.astype(F32)[:, None] * inv[None, :]
    cos, sin = jnp.cos(ang), jnp.sin(ang)
    zero = jnp.zeros_like(sin)
    rep = LANES // d
    cos_t = jnp.tile(jnp.concatenate([cos, cos], axis=1), (1, rep))
    sin_lo = jnp.tile(jnp.concatenate([-sin, zero], axis=1), (1, rep))
    sin_hi = jnp.tile(jnp.concatenate([zero, sin], axis=1), (1, rep))
    return cos_t, sin_lo, sin_hi


def _inproj_kernel(x_ref, g_ref, w_ref, c64, sl64, sh64, c32, sl32, sh32,
                   qa_ref, nsa_ref, win_ref, xb_ref, yb_ref, qc_ref, kc_ref, vc_ref, ga_ref):
    h = _rms(x_ref[...], g_ref[...]).astype(BF16)

    def seg(a, n):
        return jnp.dot(h, w_ref[:, a:a + n], preferred_element_type=F32)

    def rope(z, half, tabs):
        cos, s_lo, s_hi = tabs[0][...], tabs[1][...], tabs[2][...]
        return z * cos + pltpu.roll(z, LANES - half, 1) * s_lo + pltpu.roll(z, half, 1) * s_hi

    t64 = (c64, sl64, sh64)
    t32 = (c32, sl32, sh32)
    col = 0
    plan = ((qa_ref, (32, 32, 32, 32)), (nsa_ref, (32, None, 32, None)), (win_ref, (32, None)),
            (xb_ref, (None, None)), (yb_ref, (None, None)), (qc_ref, (16, 16)), (kc_ref, (16, 16)),
            (vc_ref, (None, None)), (ga_ref, (None,)))
    for out_ref, chunks in plan:
        z = seg(col, LANES * len(chunks))
        for c, half in enumerate(chunks):
            zc = z[:, c * LANES:(c + 1) * LANES]
            if half is not None:
                zc = rope(zc, half, t64 if half == 32 else t32)
            out_ref[:, c * LANES:(c + 1) * LANES] = zc
        col += LANES * len(chunks)


def _inproj(x2d, gain, w_r, tabs64, tabs32, seq_len, tm):
    n = x2d.shape[0]
    lt = tabs64[0].shape[0]
    if lt == 1:
        tab_spec = pl.BlockSpec((1, LANES), lambda i: (0, 0))
    else:
        per_seq = seq_len // tm
        tab_spec = pl.BlockSpec((tm, LANES), lambda i: (i % per_seq, 0))
    widths = (W_A, 4 * KVH_A * HD_A, 2 * KVH_A * HD_A, W_B, W_B, W_C, W_C, W_C, GATE_PAD)
    out_shape = tuple(jax.ShapeDtypeStruct((n, w), F32) for w in widths)
    out_specs = tuple(pl.BlockSpec((tm, w), lambda i: (i, 0)) for w in widths)
    return pl.pallas_call(
        _inproj_kernel,
        grid=(n // tm,),
        in_specs=[pl.BlockSpec((tm, D_MODEL), lambda i: (i, 0)),
                  pl.BlockSpec((1, D_MODEL), lambda i: (0, 0)),
                  pl.BlockSpec((D_MODEL, N_Z), lambda i: (0, 0))] + [tab_spec] * 6,
        out_specs=out_specs,
        out_shape=out_shape,
        compiler_params=_cparams(1),
        name="inproj",
    )(x2d, gain, w_r, *tabs64, *tabs32)


def _compress_rows(tok_refs, n_blocks, pe_ref, w1_ref, w2_ref):
    outs = []
    for kind, tok_ref in enumerate(tok_refs):
        xs = [(tok_ref[pl.ds(p, n_blocks, stride=CMP_BLOCK), :] + pe_ref[kind, p:p + 1, :]).astype(BF16)
              for p in range(CMP_BLOCK)]
        acc = jnp.dot(jnp.concatenate(xs, axis=1), w1_ref[kind], preferred_element_type=F32)
        hid = acc * jax.nn.sigmoid(acc)
        outs.append(jnp.dot(hid.astype(BF16), w2_ref[kind], preferred_element_type=F32))
    return outs


def _compress_kernel(k_ref, v_ref, pe_ref, w1_ref, w2_ref, out_ref):
    kvw = KVH_A * HD_A
    ck, cv = _compress_rows((k_ref, v_ref), out_ref.shape[0], pe_ref, w1_ref, w2_ref)
    out_ref[:, 0:kvw] = ck
    out_ref[:, kvw:2 * kvw] = cv


def _compress_weight_specs(const):
    kvw = KVH_A * HD_A
    return [pl.BlockSpec((2, CMP_BLOCK, kvw), const(3)),
            pl.BlockSpec((2, CMP_BLOCK * kvw, kvw), const(3)),
            pl.BlockSpec((2, kvw, kvw), const(3))]


def _compress(tok2d, pe2, w1bd, w2bd, t_tile):
    n_tok = tok2d.shape[0]
    kvw = KVH_A * HD_A
    return pl.pallas_call(
        _compress_kernel,
        grid=(n_tok // t_tile,),
        in_specs=[pl.BlockSpec((t_tile, kvw), lambda i: (i, 0)),
                  pl.BlockSpec((t_tile, kvw), lambda i: (i, 1))]
                 + _compress_weight_specs(lambda nd: (lambda i: (0,) * nd)),
        out_specs=pl.BlockSpec((t_tile // CMP_BLOCK, 2 * kvw), lambda i: (i, 0)),
        out_shape=jax.ShapeDtypeStruct((n_tok // CMP_BLOCK, 2 * kvw), F32),
        compiler_params=_cparams(1),
        name="compress",
    )(tok2d, tok2d, pe2, w1bd, w2bd)


def _compress_paged_kernel(layer, n_pages, chunk, pt_ref, cache_ref, pe_ref, w1_ref, w2_ref,
                           out_ref, raw, tok, sem):
    b = pl.program_id(0)
    n_chunks = n_pages // chunk
    page_rows = raw.shape[3]
    kvw = KVH_A * HD_A
    blocks_per_chunk = chunk * page_rows // CMP_BLOCK

    def page_copy(page, slot, j):
        return pltpu.make_async_copy(cache_ref.at[layer, page, pl.ds(0, 2 * kvw), :], raw.at[slot, j],
                                     sem.at[slot])

    def start(c, slot):
        for j in range(chunk):
            page_copy(pt_ref[b * n_pages + c * chunk + j], slot, j).start()

    def wait(slot):
        for j in range(chunk):
            page_copy(0, slot, j).wait()

    start(0, 0)
    for c in range(n_chunks):
        slot = c % 2
        if c + 1 < n_chunks:
            start(c + 1, 1 - slot)
        wait(slot)

        def to_token_major(j, carry):
            r0 = pl.multiple_of(j * page_rows, page_rows)
            for kind in range(2):
                tok[kind, pl.ds(r0, page_rows), :] = jnp.transpose(raw[slot, j, kind * kvw:(kind + 1) * kvw, :])
            return carry

        lax.fori_loop(0, chunk, to_token_major, 0)
        ck, cv = _compress_rows((tok.at[0], tok.at[1]), blocks_per_chunk, pe_ref, w1_ref, w2_ref)
        rows = slice(c * blocks_per_chunk, (c + 1) * blocks_per_chunk)
        out_ref[0, rows, 0:kvw] = ck
        out_ref[0, rows, kvw:2 * kvw] = cv


def _compress_paged(cache_t, layer, page_table_flat, n_seq, n_pages, pe2, w1bd, w2bd, chunk):
    page_rows = cache_t.shape[3]
    kvw = KVH_A * HD_A
    n_blocks = n_pages * page_rows // CMP_BLOCK
    grid_spec = pltpu.PrefetchScalarGridSpec(
        num_scalar_prefetch=1,
        grid=(n_seq,),
        in_specs=[pl.BlockSpec(memory_space=pl.ANY)]
                 + _compress_weight_specs(lambda nd: (lambda b, pt: (0,) * nd)),
        out_specs=pl.BlockSpec((1, n_blocks, 2 * kvw), lambda b, pt: (b, 0, 0)),
        scratch_shapes=[pltpu.VMEM((2, chunk, 2 * kvw, page_rows), F32),
                        pltpu.VMEM((2, chunk * page_rows, kvw), F32),
                        pltpu.SemaphoreType.DMA((2,))],
    )
    return pl.pallas_call(
        functools.partial(_compress_paged_kernel, layer, n_pages, chunk),
        grid_spec=grid_spec,
        out_shape=jax.ShapeDtypeStruct((n_seq, n_blocks, 2 * kvw), F32),
        compiler_params=_cparams(1),
        name="compress_paged",
    )(page_table_flat, cache_t, pe2, w1bd, w2bd)


def _cmp_branch(qg, ck_e, ck_o, cv_e, cv_o, m_e, m_o, n_rep):
    tq, n_s = m_e.shape
    single = tq == 1
    split = (lambda x: x) if single else (lambda x: x.reshape(n_rep, tq, n_s))
    merge = (lambda x: x) if single else (lambda x: x.reshape(n_rep * tq, n_s))
    if not single:
        m_e, m_o = m_e[None], m_o[None]
    s_e = jnp.where(m_e, split(_dot_nt(qg, ck_e)), NEG)
    s_o = jnp.where(m_o, split(_dot_nt(qg, ck_o)), NEG)
    mx = jnp.maximum(jnp.max(s_e, axis=-1, keepdims=True), jnp.max(s_o, axis=-1, keepdims=True))
    e_e = jnp.exp(s_e - mx)
    e_o = jnp.exp(s_o - mx)
    den = jnp.sum(e_e, axis=-1, keepdims=True) + jnp.sum(e_o, axis=-1, keepdims=True)
    p_e = jnp.where(m_e, e_e / den, 0.0)
    p_o = jnp.where(m_o, e_o / den, 0.0)
    o_c = _dot(merge(p_e), cv_e) + _dot(merge(p_o), cv_o)
    imp = jnp.sum(p_e, axis=0, keepdims=single) + jnp.sum(p_o, axis=0, keepdims=single)
    return o_c, imp


def _block_scores(imp, qpos, blk):
    cur = qpos // SEL_BLOCK
    valid = blk * SEL_BLOCK <= qpos
    forced = (blk == 0) | (blk == cur) | (blk == cur - 1)
    return jnp.where(valid, imp + jnp.where(forced, FORCE_BONUS, 0.0), -jnp.inf)


def _select_blocks(scores, n_top):
    tq, n_s = scores[0].shape
    n_g = len(scores)
    if (n_g * n_s) % LANES == 0 and tq % LANES == 0:
        st = jnp.transpose(jnp.concatenate(scores, axis=1))
        sub = _iota((n_s, tq), 0)
        sel_t = []
        for g in range(n_g):
            sg = st[g * n_s:(g + 1) * n_s]
            rank = jnp.zeros((n_s, tq), jnp.int32)
            for j in range(n_s):
                row = sg[j:j + 1, :]
                rank = rank + ((row > sg) | ((row == sg) & (sub > j))).astype(jnp.int32)
            sel_t.append((rank < n_top).astype(F32))
        sel = jnp.transpose(jnp.concatenate(sel_t, axis=0))
        return [sel[:, g * n_s:(g + 1) * n_s].astype(BF16) for g in range(n_g)]
    blk = _iota((tq, n_s), 1)
    sels = []
    for score in scores:
        rank = jnp.zeros((tq, n_s), jnp.int32)
        for j in range(n_s):
            col = score[:, j:j + 1]
            rank = rank + ((col > score) | ((col == score) & (blk > j))).astype(jnp.int32)
        sels.append((rank < n_top).astype(BF16))
    return sels


def _gate_mix_norm(o_c, o_s, o_w, gate, gn, tq):
    heads = []
    for g in range(KVH_A):
        for r in range(G_A):
            hh = g * G_A + r
            rows = slice(r * tq, (r + 1) * tq)
            o = (o_c[g][rows] * gate[:, 3 * hh:3 * hh + 1]
                 + o_s[g][rows] * gate[:, 3 * hh + 1:3 * hh + 2]
                 + o_w[g][rows] * gate[:, 3 * hh + 2:3 * hh + 3])
            heads.append(o)
    return _rms(jnp.concatenate(heads, axis=1), gn)


def _nsa_prompt_kernel(tq, tk, hp, q_ref, ga_ref, cc_ref, sk_ref, sv_ref, wk_ref, wv_ref, gn_ref, out_ref):
    i = pl.program_id(1)
    q0 = i * tq
    n_s = cc_ref.shape[3]
    seq = sk_ref.shape[1]
    n_top = min(N_SEL, n_s)
    scale = HD_A ** -0.5
    q = q_ref[...] * scale
    q2 = q_ref[...] * (scale * LOG2E)
    qpos = q0 + _iota((tq, 1), 0)
    blk = _iota((tq, n_s), 1)
    m_e = blk * SEL_BLOCK + (CMP_BLOCK - 1) <= qpos
    m_o = blk * SEL_BLOCK + (2 * CMP_BLOCK - 1) <= qpos
    n_kt = (q0 + tq - 1) // tk + 1
    w_len = WINDOW + tq
    w_start = pl.multiple_of(jnp.maximum(q0 - WINDOW, 0), tq)
    wpos = w_start + _iota((tq, w_len), 1)
    rel = qpos - wpos
    m_w = (rel >= 0) & (rel < WINDOW)
    o_c, o_w, scores, qg2s = [], [], [], []
    for g in range(KVH_A):
        qg = jnp.concatenate([q[:, (g * G_A + r) * HD_A:(g * G_A + r + 1) * HD_A] for r in range(G_A)],
                             axis=0).astype(BF16)
        oc, imp = _cmp_branch(qg, cc_ref[0, 0, g], cc_ref[0, 1, g], cc_ref[0, 0, KVH_A + g],
                              cc_ref[0, 1, KVH_A + g], m_e, m_o, G_A)
        o_c.append(oc)
        scores.append(_block_scores(imp, qpos, blk))
        qg2s.append(jnp.concatenate([q2[:, (g * G_A + r) * HD_A:(g * G_A + r + 1) * HD_A] for r in range(G_A)],
                                    axis=0).astype(BF16))

        kw = wk_ref[0, pl.ds(w_start, w_len), g * HD_A:(g + 1) * HD_A]
        vw = wv_ref[0, pl.ds(w_start, w_len), g * HD_A:(g + 1) * HD_A]
        s = jnp.where(m_w[None], _dot_nt(qg, kw).reshape(G_A, tq, w_len), NEG)
        e = jnp.exp(s - jnp.max(s, axis=-1, keepdims=True))
        p = jnp.where(m_w[None], e / jnp.sum(e, axis=-1, keepdims=True), 0.0)
        o_w.append(_dot(p.reshape(G_A * tq, w_len), vw))
    sels = _select_blocks(scores, n_top)

    chains = [(g, c) for g in range(KVH_A) for c in range(G_A // hp)]

    def body(kt, carry):
        k0 = pl.multiple_of(kt * tk, tk)
        kpos = k0 + _iota((n_s, tk), 1)
        expand = (kpos // SEL_BLOCK == _iota((n_s, tk), 0)).astype(BF16)
        causal = k0 + _iota((tq, tk), 1) <= qpos
        new = []
        for g in range(KVH_A):
            k = sk_ref[0, pl.ds(k0, tk), g * HD_A:(g + 1) * HD_A].astype(BF16)
            v = sv_ref[0, pl.ds(k0, tk), g * HD_A:(g + 1) * HD_A].astype(BF16)
            keep = ((jnp.dot(sels[g], expand, preferred_element_type=F32) > 0.5) & causal)[None]
            for c in range(G_A // hp):
                m_run, l_run, acc = carry[len(new)]
                qh = qg2s[g][c * hp * tq:(c + 1) * hp * tq]
                s = jnp.where(keep, _dot_nt(qh, k).reshape(hp, tq, tk), NEG)
                m_new = jnp.maximum(m_run, jnp.max(s, axis=-1, keepdims=True))
                alpha = jnp.exp2(m_run - m_new)
                e = jnp.exp2(s - m_new)
                l_new = alpha * l_run + jnp.sum(e, axis=-1, keepdims=True)
                acc = alpha.reshape(hp * tq, 1) * acc + _dot(e.reshape(hp * tq, tk), v)
                new.append((m_new, l_new, acc))
        return tuple(new)

    init = tuple((jnp.full((hp, tq, 1), NEG, F32), jnp.zeros((hp, tq, 1), F32),
                  jnp.zeros((hp * tq, HD_A), F32)) for _ in chains)
    fin = lax.fori_loop(0, n_kt, body, init)
    o_s = []
    for g in range(KVH_A):
        parts = [fin[j][2] / fin[j][1].reshape(hp * tq, 1) for j, (gg, _) in enumerate(chains) if gg == g]
        o_s.append(jnp.concatenate(parts, axis=0))
    gate = jax.nn.sigmoid(ga_ref[...])
    out_ref[...] = _gate_mix_norm(o_c, o_s, o_w, gate, gn_ref[...], tq)


def _nsa_prompt(qa, ga, cc, nsa3, win3, gn, tq, tk, hp):
    n = qa.shape[0]
    bsz, seq, _ = nsa3.shape
    per_seq = seq // tq
    n_s = cc.shape[3]
    kvw = KVH_A * HD_A
    return pl.pallas_call(
        functools.partial(_nsa_prompt_kernel, tq, tk, hp),
        grid=(bsz, per_seq),
        in_specs=[pl.BlockSpec((tq, W_A), lambda b, i: (b * per_seq + i, 0)),
                  pl.BlockSpec((tq, GATE_PAD), lambda b, i: (b * per_seq + i, 0)),
                  pl.BlockSpec((1, 2, 2 * KVH_A, n_s, HD_A), lambda b, i: (b, 0, 0, 0, 0)),
                  pl.BlockSpec((1, seq, kvw), lambda b, i: (b, 0, 2)),
                  pl.BlockSpec((1, seq, kvw), lambda b, i: (b, 0, 3)),
                  pl.BlockSpec((1, seq, kvw), lambda b, i: (b, 0, 0)),
                  pl.BlockSpec((1, seq, kvw), lambda b, i: (b, 0, 1)),
                  pl.BlockSpec((1, W_A), lambda b, i: (0, 0))],
        out_specs=pl.BlockSpec((tq, W_A), lambda b, i: (b * per_seq + i, 0)),
        out_shape=jax.ShapeDtypeStruct((n, W_A), F32),
        compiler_params=_cparams(2),
        name="nsa_prompt",
    )(qa, ga, cc, nsa3, nsa3, win3, win3, gn)


def _lru_gates(xc, wa_ref, ba_ref, wx_ref, bx_ref, lam_ref):
    r = jax.nn.sigmoid(_dot(xc, wa_ref[...]) + ba_ref[...])
    ig = jax.nn.sigmoid(_dot(xc, wx_ref[...]) + bx_ref[...])
    log_a = -LRU_C * r * jax.nn.softplus(-lam_ref[...])
    a = jnp.exp(log_a)
    u = jnp.sqrt(jnp.tanh(-log_a) * (a * a + 1.0)) * (ig * xc)
    return a, u


def _lru_prompt_kernel(tl, xb_ref, yb_ref, cw_ref, cb_ref, wa_ref, ba_ref, wx_ref, bx_ref, lam_ref, gn_ref,
                       ob_ref, h_ref, cs_ref, xpad, h_carry):
    i = pl.program_id(1)

    @pl.when(i == 0)
    def _():
        xpad[0:SUBLANES, :] = jnp.zeros((SUBLANES, W_B), F32)
        h_carry[...] = jnp.zeros((1, W_B), F32)

    x = xb_ref[...]
    xpad[SUBLANES:SUBLANES + tl, :] = x
    xc = cb_ref[...]
    for k in range(LRU_CONV):
        off = SUBLANES - (LRU_CONV - 1) + k
        xc = xc + xpad[off:off + tl, :] * cw_ref[k:k + 1, :]
    a, u = _lru_gates(xc, wa_ref, ba_ref, wx_ref, bx_ref, lam_ref)
    row = _iota((tl, W_B), 0)
    s = 1
    while s < tl:
        a_sh = jnp.where(row >= s, pltpu.roll(a, s, 0), 1.0)
        u_sh = jnp.where(row >= s, pltpu.roll(u, s, 0), 0.0)
        u = a * u_sh + u
        a = a * a_sh
        s *= 2
    h = a * h_carry[...] + u
    h_carry[...] = h[tl - 1:tl, :]
    xpad[0:SUBLANES, :] = x[tl - SUBLANES:tl, :]
    ob_ref[...] = _rms(_gelu(yb_ref[...]) * h, gn_ref[...])
    h_ref[0] = h[tl - 1:tl, :]
    cs_ref[0] = x[tl - (LRU_CONV - 1):tl, :]


def _lru_prompt(xb, yb, bsz, seq, cw, cb, wa, ba, wx, bx, lam, gn, tl):
    n = xb.shape[0]
    per_seq = seq // tl
    row = lambda b, i: (b * per_seq + i, 0)
    const = lambda b, i: (0, 0)
    vec = pl.BlockSpec((1, W_B), const)
    return pl.pallas_call(
        functools.partial(_lru_prompt_kernel, tl),
        grid=(bsz, per_seq),
        in_specs=[pl.BlockSpec((tl, W_B), row), pl.BlockSpec((tl, W_B), row),
                  pl.BlockSpec((LRU_CONV, W_B), const), vec,
                  pl.BlockSpec((W_B, W_B), const), vec, pl.BlockSpec((W_B, W_B), const), vec, vec, vec],
        out_specs=(pl.BlockSpec((tl, W_B), row),
                   pl.BlockSpec((1, 1, W_B), lambda b, i: (b, 0, 0)),
                   pl.BlockSpec((1, LRU_CONV - 1, W_B), lambda b, i: (b, 0, 0))),
        out_shape=(jax.ShapeDtypeStruct((n, W_B), F32),
                   jax.ShapeDtypeStruct((bsz, 1, W_B), F32),
                   jax.ShapeDtypeStruct((bsz, LRU_CONV - 1, W_B), F32)),
        scratch_shapes=[pltpu.VMEM((SUBLANES + tl, W_B), F32), pltpu.VMEM((1, W_B), F32)],
        compiler_params=_cparams(2),
        name="lru_prompt",
    )(xb, yb, cw, cb, wa, ba, wx, bx, lam, gn)


def _lru_step_kernel(xb_ref, yb_ref, buf_ref, h0_ref, cw_ref, cb_ref, wa_ref, ba_ref, wx_ref, bx_ref,
                     lam_ref, gn_ref, ob_ref, h_ref):
    x = xb_ref[...]
    xc = cb_ref[...]
    for k in range(LRU_CONV - 1):
        xc = xc + buf_ref[k] * cw_ref[k:k + 1, :]
    xc = xc + x * cw_ref[LRU_CONV - 1:LRU_CONV, :]
    a, u = _lru_gates(xc, wa_ref, ba_ref, wx_ref, bx_ref, lam_ref)
    h = a * h0_ref[...] + u
    ob_ref[...] = _rms(_gelu(yb_ref[...]) * h, gn_ref[...])
    h_ref[...] = h


def _lru_step(xb, yb, buf_t, h0, cw, cb, wa, ba, wx, bx, lam, gn):
    n = xb.shape[0]
    return pl.pallas_call(
        _lru_step_kernel,
        out_shape=(jax.ShapeDtypeStruct((n, W_B), F32), jax.ShapeDtypeStruct((n, W_B), F32)),
        compiler_params=pltpu.CompilerParams(vmem_limit_bytes=VMEM_LIMIT),
        name="lru_step",
    )(xb, yb, buf_t, h0, cw, cb, wa, ba, wx, bx, lam, gn)


def _diff_lambda(dl_ref, lam_init):
    dl = dl_ref[...]
    s01 = jnp.sum(dl[0:1, :] * dl[1:2, :], axis=-1, keepdims=True)
    s23 = jnp.sum(dl[2:3, :] * dl[3:4, :], axis=-1, keepdims=True)
    return jnp.exp(s01) - jnp.exp(s23) + lam_init


def _diff_prompt_kernel(tq, tk, lam_init, q_ref, k_ref, v_ref, dl_ref, sub_ref, out_ref):
    i = pl.program_id(1)
    q0 = i * tq
    lam = _diff_lambda(dl_ref, lam_init)
    n_full = (q0 + 1) // tk
    n_kt = (q0 + tq - 1) // tk + 1
    qpos = q0 + _iota((tq, 1), 0)
    q = (q_ref[...] * (DC ** -0.5 * LOG2E)).astype(BF16)
    maps = [(h, c) for h in range(N_C) for c in range(2)]

    def make_body(masked):
        def body(kt, carry):
            k0 = pl.multiple_of(kt * tk, tk)
            kt_all = k_ref[0, pl.ds(k0, tk), :].astype(BF16)
            vt_all = v_ref[0, pl.ds(k0, tk), :].astype(BF16)
            keep = (k0 + _iota((tq, tk), 1) <= qpos) if masked else None
            new = []
            for (h, c), (m_run, l_run, acc) in zip(maps, carry):
                lo = h * DV_C + c * DC
                s = _dot_nt(q[:, lo:lo + DC], kt_all[:, lo:lo + DC])
                if masked:
                    s = jnp.where(keep, s, NEG)
                m_new = jnp.maximum(m_run, jnp.max(s, axis=-1, keepdims=True))
                alpha = jnp.exp2(m_run - m_new)
                e = jnp.exp2(s - m_new)
                new.append((m_new, alpha * l_run + jnp.sum(e, axis=-1, keepdims=True),
                            alpha * acc + _dot(e, vt_all[:, h * DV_C:(h + 1) * DV_C])))
            return tuple(new)
        return body

    init = tuple((jnp.full((tq, 1), NEG, F32), jnp.zeros((tq, 1), F32), jnp.zeros((tq, DV_C), F32))
                 for _ in maps)
    mid = lax.fori_loop(0, n_full, make_body(False), init)
    fin = lax.fori_loop(n_full, n_kt, make_body(True), mid)
    outs = []
    for h in range(N_C):
        res = [fin[2 * h + c][2] / fin[2 * h + c][1] for c in range(2)]
        od = res[0] - lam * res[1]
        outs.append(_rms(od, sub_ref[...]) * (1.0 - lam_init))
    out_ref[...] = jnp.concatenate(outs, axis=1)


def _diff_prompt(qc, kc3, vc3, dl, sub, lam_init, tq, tk):
    n = qc.shape[0]
    bsz, seq, _ = kc3.shape
    per_seq = seq // tq
    return pl.pallas_call(
        functools.partial(_diff_prompt_kernel, tq, tk, lam_init),
        grid=(bsz, per_seq),
        in_specs=[pl.BlockSpec((tq, W_C), lambda b, i: (b * per_seq + i, 0)),
                  pl.BlockSpec((1, seq, W_C), lambda b, i: (b, 0, 0)),
                  pl.BlockSpec((1, seq, W_C), lambda b, i: (b, 0, 0)),
                  pl.BlockSpec((4, DC), lambda b, i: (0, 0)),
                  pl.BlockSpec((1, DV_C), lambda b, i: (0, 0))],
        out_specs=pl.BlockSpec((tq, W_C), lambda b, i: (b * per_seq + i, 0)),
        out_shape=jax.ShapeDtypeStruct((n, W_C), F32),
        compiler_params=_cparams(2),
        name="diff_prompt",
    )(qc, kc3, vc3, dl, sub)


def _diff_paged_kernel(kp, lam_init, pt_ref, q_ref, kn_ref, vn_ref, dl_ref, sub_ref, *rest):
    k_refs, v_refs = rest[:kp], rest[kp:2 * kp]
    out_ref, m_sc, l_sc, acc_sc = rest[2 * kp:]
    c = pl.program_id(1)
    n_rows = 2 * N_C

    @pl.when(c == 0)
    def _():
        m_sc[...] = jnp.full((n_rows, 1), NEG, F32)
        l_sc[...] = jnp.zeros((n_rows, 1), F32)
        acc_sc[...] = jnp.zeros((n_rows, W_C), F32)

    own = _iota((n_rows, W_C), 1) // DC == _iota((n_rows, W_C), 0)
    q_bd = jnp.where(own, jnp.broadcast_to(q_ref[0], (n_rows, W_C)), 0.0) * (DC ** -0.5)
    q_bf = q_bd.astype(BF16)
    s = jnp.concatenate([_dot(q_bf, k_refs[j][0]) for j in range(kp)], axis=1)
    m_run = m_sc[...]
    m_new = jnp.maximum(m_run, jnp.max(s, axis=-1, keepdims=True))
    alpha = jnp.exp(m_run - m_new)
    e = jnp.exp(s - m_new)
    page = k_refs[0].shape[2]
    acc = alpha * acc_sc[...]
    for j in range(kp):
        acc = acc + _dot_nt(e[:, j * page:(j + 1) * page], v_refs[j][0])
    l_new = alpha * l_sc[...] + jnp.sum(e, axis=-1, keepdims=True)
    m_sc[...] = m_new
    l_sc[...] = l_new
    acc_sc[...] = acc

    @pl.when(c == pl.num_programs(1) - 1)
    def _():
        s_n = jnp.sum(q_bd * kn_ref[0], axis=-1, keepdims=True)
        m_f = jnp.maximum(m_new, s_n)
        al = jnp.exp(m_new - m_f)
        e_n = jnp.exp(s_n - m_f)
        o = (al * acc + e_n * vn_ref[0]) / (al * l_new + e_n)
        lam = _diff_lambda(dl_ref, lam_init)
        lane_head = _iota((n_rows, W_C), 1) // DV_C
        rowi = _iota((n_rows, W_C), 0)
        comb = jnp.where(rowi == 2 * lane_head, o, 0.0) - lam * jnp.where(rowi == 2 * lane_head + 1, o, 0.0)
        od = jnp.sum(comb, axis=0, keepdims=True)
        heads = [_rms(od[:, h * DV_C:(h + 1) * DV_C], sub_ref[...]) * (1.0 - lam_init) for h in range(N_C)]
        out_ref[0] = jnp.concatenate(heads, axis=1)


def _diff_paged(qc, kc_new, vc_new, cache_k4, cache_v4, layer, page_table_flat, n_pages, dl, sub, lam_init, kp):
    n_seq = qc.shape[0]
    page = cache_k4.shape[3]

    def page_spec(j):
        return pl.BlockSpec((None, 1, W_C, page),
                            lambda b, c, pt: (layer, pt[b * n_pages + c * kp + j], 0, 0))

    row = pl.BlockSpec((1, 1, W_C), lambda b, c, pt: (b, 0, 0))
    grid_spec = pltpu.PrefetchScalarGridSpec(
        num_scalar_prefetch=1,
        grid=(n_seq, n_pages // kp),
        in_specs=[row, row, row,
                  pl.BlockSpec((4, DC), lambda b, c, pt: (0, 0)),
                  pl.BlockSpec((1, DV_C), lambda b, c, pt: (0, 0))]
                 + [page_spec(j) for j in range(kp)] + [page_spec(j) for j in range(kp)],
        out_specs=row,
        scratch_shapes=[pltpu.VMEM((2 * N_C, 1), F32), pltpu.VMEM((2 * N_C, 1), F32),
                        pltpu.VMEM((2 * N_C, W_C), F32)],
    )
    return pl.pallas_call(
        functools.partial(_diff_paged_kernel, kp, lam_init),
        grid_spec=grid_spec,
        out_shape=jax.ShapeDtypeStruct((n_seq, 1, W_C), F32),
        compiler_params=_cparams(2),
        name="diff_paged",
    )(page_table_flat, qc, kc_new, vc_new, dl, sub, *([cache_k4] * kp), *([cache_v4] * kp))


def _nsa_step_a_kernel(qpos, n_s, q_ref, cc_ref, wb_ref, wn_ref, oc_ref, ow_ref, idx_ref):
    nsp = cc_ref.shape[3]
    n_buf = wb_ref.shape[4]
    q = q_ref[0] * (HD_A ** -0.5)
    blk = _iota((1, nsp), 1)
    real = blk < n_s
    m_e = real & (blk * SEL_BLOCK + (CMP_BLOCK - 1) <= qpos)
    m_o = real & (blk * SEL_BLOCK + (2 * CMP_BLOCK - 1) <= qpos)
    kpos = qpos - n_buf + _iota((1, n_buf), 1)
    rel = qpos - kpos
    m_w = (rel >= 0) & (rel < WINDOW) & (kpos >= 0)
    rows = _iota((nsp, nsp), 0)
    cols = _iota((nsp, nsp), 1)
    lane = _iota((1, LANES), 1)
    oc_parts, ow_parts, idx_rows = [], [], []
    for g in range(KVH_A):
        qg = jnp.concatenate([q[:, (g * G_A + r) * HD_A:(g * G_A + r + 1) * HD_A] for r in range(G_A)], axis=0)
        oc, imp = _cmp_branch(qg, cc_ref[0, 0, g], cc_ref[0, 1, g], cc_ref[0, 0, KVH_A + g],
                              cc_ref[0, 1, KVH_A + g], m_e, m_o, G_A)
        oc_parts.append(oc)
        score = jnp.where(real, _block_scores(imp, qpos, blk), -jnp.inf)
        s_row = jnp.broadcast_to(score, (nsp, nsp))
        s_col = jnp.transpose(s_row)
        beats = (s_col > s_row) | ((s_col == s_row) & (rows < cols))
        rank = jnp.sum(beats.astype(jnp.int32), axis=0, keepdims=True)
        idx = jnp.zeros((1, LANES), jnp.int32)
        blk_f = blk.astype(F32)
        for r in range(min(N_SEL, n_s)):
            pick = jnp.sum(jnp.where(rank == r, blk_f, 0.0), axis=-1, keepdims=True).astype(jnp.int32)
            idx = jnp.where(lane == r, pick, idx)
        idx_rows.append(idx)
        kb_t = wb_ref[0, 0, g]
        vb_t = wb_ref[0, 1, g]
        kn = wn_ref[0, :, g * HD_A:(g + 1) * HD_A]
        vn = wn_ref[0, :, (KVH_A + g) * HD_A:(KVH_A + g + 1) * HD_A]
        s_b = jnp.where(m_w, _dot(qg, kb_t), NEG)
        s_n = jnp.sum(qg.astype(BF16).astype(F32) * kn.astype(BF16).astype(F32), axis=-1, keepdims=True)
        mx = jnp.maximum(jnp.max(s_b, axis=-1, keepdims=True), s_n)
        e_b = jnp.exp(s_b - mx)
        e_n = jnp.exp(s_n - mx)
        den = jnp.sum(e_b, axis=-1, keepdims=True) + e_n
        p_b = jnp.where(m_w, e_b / den, 0.0)
        ow_parts.append(_dot_nt(p_b, vb_t) + (e_n / den) * vn)
    oc_ref[0] = jnp.concatenate(oc_parts, axis=0)
    ow_ref[0] = jnp.concatenate(ow_parts, axis=0)
    idx_ref[0] = jnp.concatenate(idx_rows + [jnp.zeros((SUBLANES - KVH_A, LANES), jnp.int32)], axis=0)


def _nsa_step_a(qa3, cc, win_cache_t, layer, win_new3, qpos, n_s):
    n_seq = qa3.shape[0]
    nsp = cc.shape[3]
    n_buf = win_cache_t.shape[5]
    kvw = 2 * KVH_A * HD_A
    return pl.pallas_call(
        functools.partial(_nsa_step_a_kernel, qpos, n_s),
        grid=(n_seq,),
        in_specs=[pl.BlockSpec((1, 1, W_A), lambda b: (b, 0, 0)),
                  pl.BlockSpec((1, 2, 2 * KVH_A, nsp, HD_A), lambda b: (b, 0, 0, 0, 0)),
                  pl.BlockSpec((None, 1, 2, KVH_A, HD_A, n_buf), lambda b: (layer, b, 0, 0, 0, 0)),
                  pl.BlockSpec((1, 1, kvw), lambda b: (b, 0, 0))],
        out_specs=(pl.BlockSpec((1, H_A, HD_A), lambda b: (b, 0, 0)),
                   pl.BlockSpec((1, H_A, HD_A), lambda b: (b, 0, 0)),
                   pl.BlockSpec((1, SUBLANES, LANES), lambda b: (b, 0, 0))),
        out_shape=(jax.ShapeDtypeStruct((n_seq, H_A, HD_A), F32),
                   jax.ShapeDtypeStruct((n_seq, H_A, HD_A), F32),
                   jax.ShapeDtypeStruct((n_seq, SUBLANES, LANES), jnp.int32)),
        compiler_params=_cparams(1),
        name="nsa_step_rank",
    )(qa3, cc, win_cache_t, win_new3)


def _nsa_step_b_kernel(n_top, n_past, blocks_per_page, idx_ref, pt_ref, q_ref, oc_ref, ow_ref, ga_ref, new_ref, gn_ref, *rest):
    nk = KVH_A * n_top
    k_refs, v_refs = rest[:nk], rest[nk:2 * nk]
    out_ref = rest[2 * nk]
    b = pl.program_id(0)
    q = q_ref[0] * (HD_A ** -0.5)
    o_s = []
    for g in range(KVH_A):
        qg = jnp.concatenate([q[:, (g * G_A + r) * HD_A:(g * G_A + r + 1) * HD_A] for r in range(G_A)], axis=0)
        page = k_refs[0].shape[4]
        lane_blk = _iota((1, page), 1) // SEL_BLOCK
        s_parts = []
        has_tail = False
        for j in range(n_top):
            blk = idx_ref[(b * KVH_A + g) * n_top + j]
            s_j = _dot(qg, k_refs[g * n_top + j][0, 0, 0])
            keep = (lane_blk == blk % blocks_per_page) & (blk < n_past)
            s_parts.append(jnp.where(keep, s_j, NEG))
            has_tail = (blk == n_past) | has_tail
        s = jnp.concatenate(s_parts, axis=1)
        kn = new_ref[0, :, 2 * KVH_A * HD_A + g * HD_A:2 * KVH_A * HD_A + (g + 1) * HD_A]
        vn = new_ref[0, :, 3 * KVH_A * HD_A + g * HD_A:3 * KVH_A * HD_A + (g + 1) * HD_A]
        s_n = jnp.sum(qg.astype(BF16).astype(F32) * kn.astype(BF16).astype(F32), axis=-1, keepdims=True)
        s_n = jnp.where(has_tail, s_n, NEG)
        mx = jnp.maximum(jnp.max(s, axis=-1, keepdims=True), s_n)
        e = jnp.exp(s - mx)
        e_n = jnp.exp(s_n - mx)
        den = jnp.sum(e, axis=-1, keepdims=True) + e_n
        e = jnp.where(s > 0.5 * NEG, e, 0.0)
        e_n = jnp.where(s_n > 0.5 * NEG, e_n, 0.0)
        acc = (e_n / den) * vn
        for j in range(n_top):
            acc = acc + _dot_nt(e[:, j * page:(j + 1) * page] / den, v_refs[g * n_top + j][0, 0, 0])
        o_s.append(acc)
    oc = oc_ref[0]
    ow = ow_ref[0]
    gate = jax.nn.sigmoid(ga_ref[0])
    out_ref[0] = _gate_mix_norm([oc[0:G_A], oc[G_A:2 * G_A]], o_s, [ow[0:G_A], ow[G_A:2 * G_A]],
                                gate, gn_ref[...], 1)


def _nsa_step_select(idx_flat, page_table_flat, n_pages, cache_t6, layer,
                     qa3, oc, ow, ga3, nsa_new3, gn, n_top):
    n_seq = qa3.shape[0]
    kvw = KVH_A * HD_A
    page_rows = cache_t6.shape[5]
    blocks_per_page = page_rows // SEL_BLOCK
    n_past = n_pages * blocks_per_page

    def blk_spec(g, j, kind):
        def imap(b, idx, pt):
            blk = jnp.minimum(idx[(b * KVH_A + g) * n_top + j], n_past - 1)
            return (layer, pt[b * n_pages + blk // blocks_per_page], kind, g, 0, 0)
        return pl.BlockSpec((None, 1, 1, 1, HD_A, page_rows), imap)

    def row(w):
        return pl.BlockSpec((1, 1, w), lambda b, idx, pt: (b, 0, 0))

    head = pl.BlockSpec((1, H_A, HD_A), lambda b, idx, pt: (b, 0, 0))
    pairs = [(g, j) for g in range(KVH_A) for j in range(n_top)]
    grid_spec = pltpu.PrefetchScalarGridSpec(
        num_scalar_prefetch=2,
        grid=(n_seq,),
        in_specs=[row(W_A), head, head, row(GATE_PAD), row(4 * kvw),
                  pl.BlockSpec((1, W_A), lambda b, idx, pt: (0, 0))]
                 + [blk_spec(g, j, 2) for g, j in pairs] + [blk_spec(g, j, 3) for g, j in pairs],
        out_specs=row(W_A),
    )
    n_in = 2 * len(pairs)
    return pl.pallas_call(
        functools.partial(_nsa_step_b_kernel, n_top, n_past, blocks_per_page),
        grid_spec=grid_spec,
        out_shape=jax.ShapeDtypeStruct((n_seq, 1, W_A), F32),
        compiler_params=_cparams(1),
        name="nsa_step_select",
    )(idx_flat, page_table_flat, qa3, oc, ow, ga3, nsa_new3, gn, *([cache_t6] * n_in))


def _ffn_chunks():
    fc = 256
    assert D_FF % fc == 0
    return fc, D_FF // fc


def _ffn_tail(x1, nf_ref, wup_ref, wdn_ref, conv_fn, final_gain, act_sc):
    fc, n_fc = _ffn_chunks()
    h2 = _rms(x1, nf_ref[...]).astype(BF16)
    for c in range(n_fc):
        g = jnp.dot(h2, wup_ref[:, c * fc:(c + 1) * fc], preferred_element_type=F32)
        v = jnp.dot(h2, wup_ref[:, D_FF + c * fc:D_FF + (c + 1) * fc], preferred_element_type=F32)
        act_sc[:, c * fc:(c + 1) * fc] = (_gelu(conv_fn(c, g)) * v).astype(BF16)
    y = x1 + jnp.dot(act_sc[...], wdn_ref[...], preferred_element_type=F32)
    if final_gain is not None:
        y = _rms(y, final_gain[...])
    return y


def _mix_residual(x_ref, oa_ref, ob_ref, oc_ref, wo_ref, mix_sc):
    mix_sc[:, 0:W_A] = oa_ref[...].astype(BF16)
    mix_sc[:, W_A:W_A + W_B] = ob_ref[...].astype(BF16)
    mix_sc[:, W_A + W_B:] = oc_ref[...].astype(BF16)
    return x_ref[...] + jnp.dot(mix_sc[...], wo_ref[...], preferred_element_type=F32)


def _ffn_prompt_kernel(tm, last, x_ref, oa_ref, ob_ref, oc_ref, wo_ref, nf_ref, wup_ref, cw_ref, cb_ref,
                       wdn_ref, fin_ref, y_ref, st_ref, gpad, carry, mix_sc, act_sc):
    i = pl.program_id(1)
    fc, _ = _ffn_chunks()

    @pl.when(i == 0)
    def _():
        carry[...] = jnp.zeros(carry.shape, F32)

    def conv(c, g):
        cols = slice(c * fc, (c + 1) * fc)
        gpad[0:SUBLANES, :] = carry[:, cols]
        gpad[SUBLANES:SUBLANES + tm, :] = g
        out = cb_ref[:, cols]
        for k in range(FFN_CONV):
            off = SUBLANES - (FFN_CONV - 1) + k
            out = out + gpad[off:off + tm, :] * cw_ref[k:k + 1, cols]
        carry[:, cols] = g[tm - SUBLANES:tm, :]
        return out

    x1 = _mix_residual(x_ref, oa_ref, ob_ref, oc_ref, wo_ref, mix_sc)
    y_ref[...] = _ffn_tail(x1, nf_ref, wup_ref, wdn_ref, conv, fin_ref if last else None, act_sc)
    st_ref[0] = carry[SUBLANES - (FFN_CONV - 1):SUBLANES, :]


def _ffn_prompt(x2d, oa, ob, oc, wo, nf, wup, cw, cb, wdn, fin, bsz, seq, tm, last):
    n = x2d.shape[0]
    per_seq = seq // tm
    fc, _ = _ffn_chunks()
    row = lambda b, i: (b * per_seq + i, 0)
    const = lambda b, i: (0, 0)
    one = pl.Buffered(1)
    return pl.pallas_call(
        functools.partial(_ffn_prompt_kernel, tm, last),
        grid=(bsz, per_seq),
        in_specs=[pl.BlockSpec((tm, D_MODEL), row), pl.BlockSpec((tm, W_A), row),
                  pl.BlockSpec((tm, W_B), row), pl.BlockSpec((tm, W_C), row),
                  pl.BlockSpec((D_MODEL, D_MODEL), const, pipeline_mode=one),
                  pl.BlockSpec((1, D_MODEL), const),
                  pl.BlockSpec((D_MODEL, 2 * D_FF), const, pipeline_mode=one),
                  pl.BlockSpec((FFN_CONV, D_FF), const), pl.BlockSpec((1, D_FF), const),
                  pl.BlockSpec((D_FF, D_MODEL), const, pipeline_mode=one),
                  pl.BlockSpec((1, D_MODEL), const)],
        out_specs=(pl.BlockSpec((tm, D_MODEL), row),
                   pl.BlockSpec((1, FFN_CONV - 1, D_FF), lambda b, i: (b, 0, 0))),
        out_shape=(jax.ShapeDtypeStruct((n, D_MODEL), F32),
                   jax.ShapeDtypeStruct((bsz, FFN_CONV - 1, D_FF), F32)),
        scratch_shapes=[pltpu.VMEM((SUBLANES + tm, fc), F32), pltpu.VMEM((SUBLANES, D_FF), F32),
                        pltpu.VMEM((tm, D_MODEL), BF16), pltpu.VMEM((tm, D_FF), BF16)],
        compiler_params=_cparams(2),
        name="ffn_prompt",
    )(x2d, oa, ob, oc, wo, nf, wup, cw, cb, wdn, fin)


def _ffn_step_kernel(last, x_ref, oa_ref, ob_ref, oc_ref, wo_ref, nf_ref, wup_ref, cw_ref, cb_ref, wdn_ref,
                     fin_ref, buf_ref, y_ref, g_ref, mix_sc, act_sc):
    fc, _ = _ffn_chunks()

    def conv(c, g):
        cols = slice(c * fc, (c + 1) * fc)
        g_ref[:, cols] = g
        out = cb_ref[:, cols]
        for k in range(FFN_CONV - 1):
            out = out + buf_ref[k, :, cols] * cw_ref[k:k + 1, cols]
        return out + g * cw_ref[FFN_CONV - 1:FFN_CONV, cols]

    x1 = _mix_residual(x_ref, oa_ref, ob_ref, oc_ref, wo_ref, mix_sc)
    y_ref[...] = _ffn_tail(x1, nf_ref, wup_ref, wdn_ref, conv, fin_ref if last else None, act_sc)


def _ffn_step(x2d, oa, ob, oc, wo, nf, wup, cw, cb, wdn, fin, buf_t, last):
    n = x2d.shape[0]
    return pl.pallas_call(
        functools.partial(_ffn_step_kernel, last),
        out_shape=(jax.ShapeDtypeStruct((n, D_MODEL), F32), jax.ShapeDtypeStruct((n, D_FF), F32)),
        scratch_shapes=[pltpu.VMEM((n, D_MODEL), BF16), pltpu.VMEM((n, D_FF), BF16)],
        compiler_params=pltpu.CompilerParams(vmem_limit_bytes=VMEM_LIMIT),
        name="ffn_step",
    )(x2d, oa, ob, oc, wo, nf, wup, cw, cb, wdn, fin, buf_t)


def _block_diag(blocks):
    nb, k, n = blocks.shape[-3:]
    eye = jnp.eye(nb, dtype=blocks.dtype)
    out = blocks[..., :, :, None, :] * eye[:, None, :, None]
    return out.reshape(blocks.shape[:-3] + (nb * k, nb * n))


def _layer_weights(l, w_in, cmp_pe, cmp_w1, cmp_w2, lru_gate_a_w, lru_gate_x_w, w_out, w_up, w_down):
    w = w_in[l]
    g0 = W_A + 3 * 2 * KVH_A * HD_A
    w_r = jnp.concatenate([w[:, :g0], w[:, g0 + N_GATE:], w[:, g0:g0 + N_GATE],
                           jnp.zeros((D_MODEL, GATE_PAD - N_GATE), F32)], axis=1).astype(BF16)
    pe = cmp_pe[l]
    pe4 = jnp.concatenate([pe] * KVH_A, axis=-1)
    w1 = cmp_w1[l]
    w1bd = _block_diag(jnp.stack([w1] * KVH_A, axis=2)).astype(BF16)
    w1bd = w1bd.reshape(2, CMP_BLOCK * KVH_A * HD_A, KVH_A * HD_A)
    w2 = cmp_w2[l]
    w2bd = _block_diag(jnp.stack([w2] * KVH_A, axis=1)).astype(BF16)
    wa = _block_diag(lru_gate_a_w[l]).astype(BF16)
    wx = _block_diag(lru_gate_x_w[l]).astype(BF16)
    return dict(w_r=w_r, pe4=pe4, w1bd=w1bd, w2bd=w2bd, wa=wa, wx=wx,
                wo=w_out[l].astype(BF16), wup=w_up[l].astype(BF16), wdn=w_down[l].astype(BF16))


def _split_compressed(c, n_s_pad):
    bsz, n_c, _ = c.shape
    n_s = n_c // 2
    cc = c.reshape(bsz, n_s, 2, 2 * KVH_A, HD_A).transpose(0, 2, 3, 1, 4)
    if n_s_pad > n_s:
        cc = jnp.pad(cc, ((0, 0), (0, 0), (0, 0), (0, n_s_pad - n_s), (0, 0)))
    return cc


def kernel(x_prompt, x_sample, cache_nsa_kv, cache_diff_k, cache_diff_v, cache_win_kv, state_lru_h,
           state_lru_conv, state_ffn_conv, page_table, norm_mix, w_in, cmp_pe, cmp_w1, cmp_w2, gn_nsa,
           lru_conv_w, lru_conv_b, lru_gate_a_w, lru_gate_a_b, lru_gate_x_w, lru_gate_x_b, lru_lambda,
           gn_lru, diff_lambda, diff_subln, w_out, norm_ffn, w_up, ffn_conv_w, ffn_conv_b, w_down, norm_final):
    bsz, seq, _ = x_prompt.shape
    n_seq, dec_len, _ = x_sample.shape
    depth, n_pool, page, _, _, _ = cache_nsa_kv.shape
    n_pages = page_table.shape[1]
    past_len = n_pages * page
    assert dec_len == 1 and seq % SEL_BLOCK == 0 and past_len % SEL_BLOCK == 0
    assert seq >= WINDOW + 128 and cache_win_kv.shape[2] == min(WINDOW, past_len)
    n_tok = bsz * seq
    kvw = KVH_A * HD_A

    cache_t6 = jnp.transpose(cache_nsa_kv, (0, 1, 3, 4, 5, 2))
    cache_tok = cache_t6.reshape(depth, n_pool, 4 * kvw, page)
    cache_dk = jnp.transpose(cache_diff_k, (0, 1, 3, 4, 5, 2)).reshape(depth, n_pool, W_C, page)
    cache_dv = jnp.transpose(cache_diff_v, (0, 1, 3, 4, 2)).reshape(depth, n_pool, W_C, page)
    cache_win = jnp.transpose(cache_win_kv, (0, 1, 3, 4, 5, 2))
    pt_flat = page_table.reshape(-1)

    tabs_p64 = _rope_tables(jnp.arange(seq), HD_A)
    tabs_p32 = _rope_tables(jnp.arange(seq), DC)
    pos_s = past_len + jnp.arange(1)
    tabs_s64 = _rope_tables(pos_s, HD_A)
    tabs_s32 = _rope_tables(pos_s, DC)

    n_s_dec = (past_len + SEL_BLOCK) // SEL_BLOCK
    n_s_pad = -(-n_s_dec // LANES) * LANES
    n_top_dec = min(N_SEL, n_s_dec)

    xp = x_prompt.reshape(n_tok, D_MODEL)
    xs = x_sample.reshape(n_seq, D_MODEL)
    fin = norm_final.reshape(1, D_MODEL)
    st_p = [[] for _ in range(7)]
    st_s = [[] for _ in range(7)]
    for l in range(depth):
        last = l == depth - 1
        lam_init = 0.8 - 0.6 * math.exp(-0.3 * l)
        lw = _layer_weights(l, w_in, cmp_pe, cmp_w1, cmp_w2, lru_gate_a_w, lru_gate_x_w, w_out, w_up, w_down)
        gain = norm_mix[l].reshape(1, D_MODEL)
        gn = gn_nsa[l].reshape(1, W_A)
        lru_args = (lru_conv_w[l], lru_conv_b[l].reshape(1, W_B), lw["wa"], lru_gate_a_b[l].reshape(1, W_B),
                    lw["wx"], lru_gate_x_b[l].reshape(1, W_B), lru_lambda[l].reshape(1, W_B),
                    gn_lru[l].reshape(1, W_B))
        dl = diff_lambda[l]
        sub = diff_subln[l].reshape(1, DV_C)
        ffn_args = (lw["wo"], norm_ffn[l].reshape(1, D_MODEL), lw["wup"], ffn_conv_w[l],
                    ffn_conv_b[l].reshape(1, D_FF), lw["wdn"], fin)

        qa, nsa, win, xb, yb, qc, kc, vc, ga = _inproj(xp, gain, lw["w_r"], tabs_p64, tabs_p32, seq, 512)
        comp = _compress(nsa, lw["pe4"], lw["w1bd"], lw["w2bd"], math.gcd(n_tok, 4096))
        cc = _split_compressed(comp.reshape(bsz, seq // CMP_BLOCK, 4 * HD_A), seq // SEL_BLOCK)
        nsa3 = nsa.reshape(bsz, seq, 4 * kvw)
        win3 = win.reshape(bsz, seq, 2 * kvw)
        oa = _nsa_prompt(qa, ga, cc, nsa3, win3, gn, 256, math.gcd(seq, 512), G_A // 2)
        ob, h_last, conv_st = _lru_prompt(xb, yb, bsz, seq, *lru_args, 256)
        kc3 = kc.reshape(bsz, seq, W_C)
        vc3 = vc.reshape(bsz, seq, W_C)
        oc = _diff_prompt(qc, kc3, vc3, dl, sub, lam_init, 512, 512)
        xp, ffn_st = _ffn_prompt(xp, oa, ob, oc, *ffn_args, bsz, seq, 256, last)
        n_win = min(WINDOW, seq)
        st_p[0].append(nsa.reshape(bsz, seq, 4, KVH_A, HD_A))
        st_p[1].append(win3[:, seq - n_win:].reshape(bsz, n_win, 2, KVH_A, HD_A))
        st_p[2].append(kc.reshape(bsz, seq, N_C, 2, DC))
        st_p[3].append(vc.reshape(bsz, seq, N_C, DV_C))
        st_p[4].append(h_last.reshape(bsz, W_B))
        st_p[5].append(conv_st)
        st_p[6].append(ffn_st)

        qa, nsa, win, xb, yb, qc, kc, vc, ga = _inproj(xs, gain, lw["w_r"], tabs_s64, tabs_s32, 1, n_seq)
        comp_past = _compress_paged(cache_tok, l, pt_flat, n_seq, n_pages, lw["pe4"], lw["w1bd"], lw["w2bd"],
                                    math.gcd(n_pages, 32))
        tail = jnp.pad(nsa.reshape(n_seq, 1, 4 * kvw), ((0, 0), (0, SEL_BLOCK - 1), (0, 0)))
        comp_tail = _compress(tail.reshape(n_seq * SEL_BLOCK, 4 * kvw), lw["pe4"], lw["w1bd"], lw["w2bd"],
                              n_seq * SEL_BLOCK)
        comp = jnp.concatenate([comp_past, comp_tail.reshape(n_seq, SEL_BLOCK // CMP_BLOCK, 4 * HD_A)], axis=1)
        cc = _split_compressed(comp, n_s_pad)
        qa3 = qa.reshape(n_seq, 1, W_A)
        o_cmp, o_win, idx = _nsa_step_a(qa3, cc, cache_win, l, win.reshape(n_seq, 1, 2 * kvw), past_len, n_s_dec)
        idx_flat = idx[:, :KVH_A, :n_top_dec].reshape(-1)
        oa = _nsa_step_select(idx_flat, pt_flat, n_pages, cache_t6, l, qa3, o_cmp, o_win,
                              ga.reshape(n_seq, 1, GATE_PAD), nsa.reshape(n_seq, 1, 4 * kvw), gn,
                              n_top_dec).reshape(n_seq, W_A)
        buf_t = jnp.transpose(state_lru_conv[l], (1, 0, 2))
        ob, h_new = _lru_step(xb, yb, buf_t, state_lru_h[l], *lru_args)
        oc = _diff_paged(qc.reshape(n_seq, 1, W_C), kc.reshape(n_seq, 1, W_C), vc.reshape(n_seq, 1, W_C),
                         cache_dk, cache_dv, l, pt_flat, n_pages, dl, sub, lam_init,
                         math.gcd(n_pages, 16)).reshape(n_seq, W_C)
        fbuf_t = jnp.transpose(state_ffn_conv[l], (1, 0, 2))
        xs, g_new = _ffn_step(xs, oa, ob, oc, *ffn_args, fbuf_t, last)
        st_s[0].append(nsa.reshape(n_seq, 1, 4, KVH_A, HD_A))
        st_s[1].append(win.reshape(n_seq, 1, 2, KVH_A, HD_A))
        st_s[2].append(kc.reshape(n_seq, 1, N_C, 2, DC))
        st_s[3].append(vc.reshape(n_seq, 1, N_C, DV_C))
        st_s[4].append(h_new)
        st_s[5].append(jnp.concatenate([state_lru_conv[l][:, 1:], xb[:, None, :]], axis=1))
        st_s[6].append(jnp.concatenate([state_ffn_conv[l][:, 1:], g_new[:, None, :]], axis=1))

    y_prompt = xp.reshape(bsz, seq, D_MODEL)
    y_sample = xs.reshape(n_seq, 1, D_MODEL)
    return (y_prompt, y_sample) + tuple(jnp.stack(s, axis=0) for s in st_p) + tuple(
        jnp.stack(s, axis=0) for s in st_s)
```
